```python
import jax, jax.numpy as jnp
from jax import lax
import numpy as np

D_MODEL = 4096
BATCH = 2
SEQ = 8192
DEPTH = 1

MIX_WIDTH = D_MODEL
HEAD_DIM = 128
ATTN_WIDTH = MIX_WIDTH // 2
N_ATTN_HEADS = ATTN_WIDTH // HEAD_DIM
LRU_WIDTH = MIX_WIDTH - ATTN_WIDTH
N_LRU_BLOCKS = 16
LRU_BLOCK = LRU_WIDTH // N_LRU_BLOCKS
CONV_WIDTH = 4
RG_C = 8.0
D_FF = ((8 * D_MODEL // 3 + 255) // 256) * 256
Q_BLOCK = 128
EPS = 1e-6
FORGET_BIAS_INIT = 2.0

N_IN = 3 * ATTN_WIDTH + N_ATTN_HEADS + 2 * LRU_WIDTH
SPLIT_POINTS = (
    ATTN_WIDTH,
    2 * ATTN_WIDTH,
    3 * ATTN_WIDTH,
    3 * ATTN_WIDTH + N_ATTN_HEADS,
    3 * ATTN_WIDTH + N_ATTN_HEADS + LRU_WIDTH,
)

kernel_name = "fox_rglru_parallel_hybrid"


def rms_norm(x, gain):
    xf = x.astype(jnp.float32)
    return xf * lax.rsqrt(jnp.mean(xf * xf, axis=-1, keepdims=True) + EPS) * gain.astype(jnp.float32)


def fox_attention(q, k, v, f_logit, b_f, g_q, g_k):
    B, S, _ = q.shape
    H, D = N_ATTN_HEADS, HEAD_DIM
    q = rms_norm(q.reshape(B, S, H, D), g_q) * (D ** -0.5)
    k = rms_norm(k.reshape(B, S, H, D), g_k)
    v = v.reshape(B, S, H, D).astype(jnp.float32)
    log_f = jax.nn.log_sigmoid(f_logit.astype(jnp.float32) + b_f.astype(jnp.float32))
    c = jnp.cumsum(log_f, axis=1).transpose(0, 2, 1)
    q = q.transpose(0, 2, 1, 3)
    k = k.transpose(0, 2, 1, 3)
    v = v.transpose(0, 2, 1, 3)
    n_blocks = S // Q_BLOCK
    q_blocks = q.reshape(B, H, n_blocks, Q_BLOCK, D).transpose(2, 0, 1, 3, 4)
    c_blocks = c.reshape(B, H, n_blocks, Q_BLOCK).transpose(2, 0, 1, 3)
    k_pos = jnp.arange(S)

    def one_block(args):
        q_i, c_i, start = args
        s = jnp.einsum('bhqd,bhkd->bhqk', q_i, k) + c_i[..., None] - c[:, :, None, :]
        q_pos = start + jnp.arange(Q_BLOCK)
        s = jnp.where(k_pos[None, :] <= q_pos[:, None], s, -jnp.inf)
        p = jax.nn.softmax(s, axis=-1)
        return jnp.einsum('bhqk,bhkd->bhqd', p, v)

    o = lax.map(one_block, (q_blocks, c_blocks, jnp.arange(n_blocks) * Q_BLOCK))
    return o.transpose(1, 0, 3, 2, 4).reshape(B, S, H * D)


def rglru_branch(u_x, u_y, conv_w, conv_b, w_rg, b_rg, w_ig, b_ig, lam):
    B, S, W = u_x.shape
    f32 = jnp.float32
    xr = lax.conv_general_dilated(
        u_x.astype(f32), conv_w[:, None, :].astype(f32), window_strides=(1,),
        padding=[(CONV_WIDTH - 1, 0)], dimension_numbers=('NWC', 'WIO', 'NWC'),
        feature_group_count=W) + conv_b.astype(f32)
    xh = xr.reshape(B, S, N_LRU_BLOCKS, LRU_BLOCK)
    r = jax.nn.sigmoid(jnp.einsum('bsni,nij->bsnj', xh, w_rg.astype(f32)).reshape(B, S, W) + b_rg.astype(f32))
    i = jax.nn.sigmoid(jnp.einsum('bsni,nij->bsnj', xh, w_ig.astype(f32)).reshape(B, S, W) + b_ig.astype(f32))
    log_a = -RG_C * r * jax.nn.softplus(-lam.astype(f32))
    a = jnp.exp(log_a)
    b = jnp.sqrt(-jnp.expm1(2.0 * log_a)) * (i * xr)

    def combine(left, right):
        a_l, b_l = left
        a_r, b_r = right
        return a_l * a_r, a_r * b_l + b_r

    _, h = lax.associative_scan(combine, (a, b), axis=1)
    return jax.nn.gelu(u_y.astype(f32)) * h


def hybrid_layer(x, g_mix, w_in, b_f, g_q, g_k, conv_w, conv_b, w_rg, b_rg, w_ig, b_ig, lam,
                 g_attn_out, g_lru_out, w_out, g_ffn, w_gate, w_up, w_down):
    dt = x.dtype
    xn = rms_norm(x, g_mix).astype(dt)
    proj = xn @ w_in
    q, k, v, f_logit, u_x, u_y = jnp.split(proj, SPLIT_POINTS, axis=-1)
    attn = fox_attention(q, k, v, f_logit, b_f, g_q, g_k)
    lru = rglru_branch(u_x, u_y, conv_w, conv_b, w_rg, b_rg, w_ig, b_ig, lam)
    mixed = jnp.concatenate([rms_norm(attn, g_attn_out), rms_norm(lru, g_lru_out)], axis=-1).astype(dt)
    h = x + mixed @ w_out
    hn = rms_norm(h, g_ffn).astype(dt)
    ff = (jax.nn.silu(hn @ w_gate) * (hn @ w_up)) @ w_down
    return h + ff


def setup_inputs(seed: int = 0) -> dict:
    key = jax.random.key(seed)
    ks = jax.random.split(key, 21)
    L = DEPTH
    f32 = jnp.float32

    def nrm(k, shape, scale):
        return jax.random.normal(k, shape, f32) * scale

    u = jax.random.uniform(ks[12], (L, LRU_WIDTH), f32, minval=0.9, maxval=0.999)
    p = u ** (1.0 / RG_C)
    lam = jnp.log(p) - jnp.log1p(-p)
    return {
        "x": nrm(ks[0], (BATCH, SEQ, D_MODEL), 1.0),
        "g_mix": 1.0 + nrm(ks[1], (L, D_MODEL), 0.02),
        "w_in": nrm(ks[2], (L, D_MODEL, N_IN), D_MODEL ** -0.5),
        "b_f": FORGET_BIAS_INIT + nrm(ks[3], (L, N_ATTN_HEADS), 0.5),
        "g_q": 1.0 + nrm(ks[4], (L, HEAD_DIM), 0.02),
        "g_k": 1.0 + nrm(ks[5], (L, HEAD_DIM), 0.02),
        "conv_w": nrm(ks[6], (L, CONV_WIDTH, LRU_WIDTH), CONV_WIDTH ** -0.5),
        "conv_b": nrm(ks[7], (L, LRU_WIDTH), 0.01),
        "w_rg": nrm(ks[8], (L, N_LRU_BLOCKS, LRU_BLOCK, LRU_BLOCK), LRU_BLOCK ** -0.5),
        "b_rg": nrm(ks[9], (L, LRU_WIDTH), 0.01),
        "w_ig": nrm(ks[10], (L, N_LRU_BLOCKS, LRU_BLOCK, LRU_BLOCK), LRU_BLOCK ** -0.5),
        "b_ig": nrm(ks[11], (L, LRU_WIDTH), 0.01),
        "lam": lam,
        "g_attn_out": 1.0 + nrm(ks[13], (L, ATTN_WIDTH), 0.02),
        "g_lru_out": 1.0 + nrm(ks[14], (L, LRU_WIDTH), 0.02),
        "w_out": nrm(ks[15], (L, MIX_WIDTH, D_MODEL), MIX_WIDTH ** -0.5),
        "g_ffn": 1.0 + nrm(ks[16], (L, D_MODEL), 0.02),
        "w_gate": nrm(ks[17], (L, D_MODEL, D_FF), D_MODEL ** -0.5),
        "w_up": nrm(ks[18], (L, D_MODEL, D_FF), D_MODEL ** -0.5),
        "w_down": nrm(ks[19], (L, D_FF, D_MODEL), D_FF ** -0.5),
    }


def reference(x, g_mix, w_in, b_f, g_q, g_k, conv_w, conv_b, w_rg, b_rg, w_ig, b_ig, lam,
              g_attn_out, g_lru_out, w_out, g_ffn, w_gate, w_up, w_down):
    h = x
    for l in range(DEPTH):
        h = hybrid_layer(h, g_mix[l], w_in[l], b_f[l], g_q[l], g_k[l], conv_w[l], conv_b[l],
                         w_rg[l], b_rg[l], w_ig[l], b_ig[l], lam[l], g_attn_out[l], g_lru_out[l],
                         w_out[l], g_ffn[l], w_gate[l], w_up[l], w_down[l])
    return h
```

```python
import functools
import math

import jax
import jax.numpy as jnp
from jax import lax
from jax.experimental import pallas as pl
from jax.experimental.pallas import tpu as pltpu

EPS = 1e-6
RG_C = 8.0
CONV_WIDTH = 4

V7X_LANES = 128
V7X_SUBLANES = 8
V7X_VMEM_LIMIT_BYTES = 56 * 1024 * 1024

F32 = jnp.float32
BF16 = jnp.bfloat16


def _tile(dim, pref):
    t = min(dim, pref)
    while dim % t:
        t //= 2
    return t


def _params(*sem):
    return pltpu.CompilerParams(dimension_semantics=sem, vmem_limit_bytes=V7X_VMEM_LIMIT_BYTES)


def _rms_scale(x):
    return lax.rsqrt(jnp.mean(x * x, axis=-1, keepdims=True) + EPS)


def _inproj_kernel(x_ref, gmix_ref, w_ref, wf_ref, gain_ref, proj_ref, f_ref, xn_sc, *, n_norm_tiles, head_dim):
    j = pl.program_id(1)

    @pl.when(j == 0)
    def _():
        x = x_ref[...]
        xn = (x * _rms_scale(x) * gmix_ref[...]).astype(BF16)
        xn_sc[...] = xn
        f_ref[...] = jnp.dot(xn, wf_ref[...], preferred_element_type=F32)

    y = jnp.dot(xn_sc[...], w_ref[...], preferred_element_type=F32)

    @pl.when(j < n_norm_tiles)
    def _():
        for c in range(0, y.shape[1], head_dim):
            ys = y[:, c:c + head_dim]
            proj_ref[:, c:c + head_dim] = (ys * _rms_scale(ys) * gain_ref[:, c:c + head_dim]).astype(BF16)

    @pl.when(j >= n_norm_tiles)
    def _():
        proj_ref[...] = y.astype(BF16)


def _in_projection(x2, g_mix, w_main, w_f, gain, *, n_norm_cols, head_dim):
    m, d = x2.shape
    n = w_main.shape[1]
    tm = _tile(m, 512)
    tn = _tile(math.gcd(n, n_norm_cols), 1024)
    kern = functools.partial(_inproj_kernel, n_norm_tiles=n_norm_cols // tn, head_dim=head_dim)
    return pl.pallas_call(
        kern,
        grid=(m // tm, n // tn),
        in_specs=[
            pl.BlockSpec((tm, d), lambda i, j: (i, 0)),
            pl.BlockSpec((1, d), lambda i, j: (0, 0)),
            pl.BlockSpec((d, tn), lambda i, j: (0, j)),
            pl.BlockSpec((d, V7X_LANES), lambda i, j: (0, 0)),
            pl.BlockSpec((1, tn), lambda i, j: (0, j)),
        ],
        out_specs=[
            pl.BlockSpec((tm, tn), lambda i, j: (i, j)),
            pl.BlockSpec((tm, V7X_LANES), lambda i, j: (i, 0)),
        ],
        out_shape=[
            jax.ShapeDtypeStruct((m, n), BF16),
            jax.ShapeDtypeStruct((m, V7X_LANES), F32),
        ],
        scratch_shapes=[pltpu.VMEM((tm, d), BF16)],
        compiler_params=_params("parallel", "arbitrary"),
        name="in_projection",
    )(x2, g_mix, w_main, w_f, gain)


def _forget_cumsum_kernel(f_ref, bf_ref, c_ref):
    z = f_ref[0] + bf_ref[...]
    c = jnp.minimum(z, 0.0) - jnp.log1p(jnp.exp(-jnp.abs(z)))
    row = lax.broadcasted_iota(jnp.int32, c.shape, 0)
    d = 1
    while d < c.shape[0]:
        c = c + jnp.where(row >= d, pltpu.roll(c, d, axis=0), 0.0)
        d *= 2
    c_ref[0] = c


def _forget_cumsum(f3, b_f_pad):
    b, s, lanes = f3.shape
    return pl.pallas_call(
        _forget_cumsum_kernel,
        grid=(b,),
        in_specs=[
            pl.BlockSpec((1, s, lanes), lambda i: (i, 0, 0)),
            pl.BlockSpec((1, lanes), lambda i: (0, 0)),
        ],
        out_specs=pl.BlockSpec((1, s, lanes), lambda i: (i, 0, 0)),
        out_shape=jax.ShapeDtypeStruct((b, s, lanes), F32),
        compiler_params=_params("parallel"),
        name="forget_cumsum",
    )(f3, b_f_pad)


def _attn_kernel(q_ref, k_ref, v_ref, ccol_ref, crow_ref, o_ref, m_sc, l_sc, acc_sc, *, tile):
    h = pl.program_id(1)
    qi = pl.program_id(2)
    q = q_ref[...]
    lane = lax.broadcasted_iota(jnp.int32, ccol_ref.shape[1:], 1)
    c_t = jnp.sum(jnp.where(lane == h, ccol_ref[0], 0.0), axis=1, keepdims=True)

    m_sc[...] = jnp.full(m_sc.shape, -jnp.inf, F32)
    l_sc[...] = jnp.zeros(l_sc.shape, F32)
    acc_sc[...] = jnp.zeros(acc_sc.shape, F32)

    def step(kb, masked):
        k0 = pl.multiple_of(kb * tile, tile)
        k = k_ref[pl.ds(k0, tile), :]
        v = v_ref[pl.ds(k0, tile), :]
        s = lax.dot_general(q, k, (((1,), (1,)), ((), ())), preferred_element_type=F32)
        s = s + (c_t - crow_ref[0, :, pl.ds(k0, tile)])
        if masked:
            r = lax.broadcasted_iota(jnp.int32, s.shape, 0)
            c = lax.broadcasted_iota(jnp.int32, s.shape, 1)
            s = jnp.where(c <= r, s, -jnp.inf)
        m_prev = m_sc[...]
        m_new = jnp.maximum(m_prev, jnp.max(s, axis=1, keepdims=True))
        alpha = jnp.exp(m_prev - m_new)
        p = jnp.exp(s - m_new)
        l_sc[...] = alpha * l_sc[...] + jnp.sum(p, axis=1, keepdims=True)
        acc_sc[...] = alpha * acc_sc[...] + jnp.dot(p.astype(BF16), v, preferred_element_type=F32)
        m_sc[...] = m_new

    def full_block(kb, carry):
        step(kb, False)
        return carry

    lax.fori_loop(0, qi, full_block, 0)
    step(qi, True)
    o_ref[...] = acc_sc[...] / l_sc[...]


def _attention(proj, c_col, c_row, *, batch, seq, n_heads, head_dim):
    tile = _tile(seq, 512)
    nq = seq // tile
    attn_w = n_heads * head_dim
    kern = functools.partial(_attn_kernel, tile=tile)
    return pl.pallas_call(
        kern,
        grid=(batch, n_heads, nq),
        in_specs=[
            pl.BlockSpec((tile, head_dim), lambda b, h, i: (b * nq + i, h)),
            pl.BlockSpec((seq, head_dim), lambda b, h, i: (b, n_heads + h)),
            pl.BlockSpec((seq, head_dim), lambda b, h, i: (b, 2 * n_heads + h)),
            pl.BlockSpec((1, tile, V7X_LANES), lambda b, h, i: (b, i, 0)),
            pl.BlockSpec((1, 1, seq), lambda b, h, i: (b * n_heads + h, 0, 0)),
        ],
        out_specs=pl.BlockSpec((tile, head_dim), lambda b, h, i: (b * nq + i, h)),
        out_shape=jax.ShapeDtypeStruct((batch * seq, attn_w), F32),
        scratch_shapes=[
            pltpu.VMEM((tile, 1), F32),
            pltpu.VMEM((tile, 1), F32),
            pltpu.VMEM((tile, head_dim), F32),
        ],
        compiler_params=_params("parallel", "parallel", "arbitrary"),
        name="fox_attention",
    )(proj, proj, proj, c_col, c_row)


def _lru_kernel(ux_ref, uy_ref, cw_ref, cb_ref, wg_ref, brg_ref, big_ref, lam_ref, gout_ref, o_ref,
                tail_sc, hlast_sc, xr_sc, a_sc, b_sc, *, block):
    t = pl.program_id(1)
    rows, width = xr_sc.shape
    sub = V7X_SUBLANES

    @pl.when(t == 0)
    def _():
        tail_sc[...] = jnp.zeros(tail_sc.shape, F32)
        hlast_sc[...] = jnp.zeros(hlast_sc.shape, F32)

    u = ux_ref[...].astype(F32)
    cw = cw_ref[...]
    cb = cb_ref[...]
    xr = cw[CONV_WIDTH - 1:CONV_WIDTH, :] * u + cb
    for d in range(1, CONV_WIDTH):
        xr = xr + cw[CONV_WIDTH - 1 - d:CONV_WIDTH - d, :] * pltpu.roll(u, d, axis=0)
    xr_sc[...] = xr
    u_head = u[0:sub, :]
    tail = tail_sc[...]
    row8 = lax.broadcasted_iota(jnp.int32, (sub, width), 0)
    xr_head = cw[CONV_WIDTH - 1:CONV_WIDTH, :] * u_head + cb
    for d in range(1, CONV_WIDTH):
        shifted = jnp.where(row8 < d, pltpu.roll(tail, d, axis=0), pltpu.roll(u_head, d, axis=0))
        xr_head = xr_head + cw[CONV_WIDTH - 1 - d:CONV_WIDTH - d, :] * shifted
    xr_sc[0:sub, :] = xr_head
    tail_sc[...] = u[rows - sub:rows, :]

    lam = lam_ref[...]
    neg_c_softplus = -RG_C * (jnp.maximum(-lam, 0.0) + jnp.log1p(jnp.exp(-jnp.abs(lam))))

    sub_idx = lax.broadcasted_iota(jnp.int32, (rows // sub, sub, block), 1)
    for n in range(width // block):
        cs = slice(n * block, (n + 1) * block)
        xb = xr_sc[:, cs]
        g = jnp.dot(xb.astype(BF16), wg_ref[n], preferred_element_type=F32)
        r = 1.0 / (1.0 + jnp.exp(-(g[:, :block] + brg_ref[:, cs])))
        i = 1.0 / (1.0 + jnp.exp(-(g[:, block:] + big_ref[:, cs])))
        log_a = r * neg_c_softplus[:, cs]
        a = jnp.exp(log_a)
        b = jnp.sqrt(1.0 - a * a) * (i * xb)
        a3 = a.reshape(rows // sub, sub, block)
        b3 = b.reshape(rows // sub, sub, block)
        d = 1
        while d < sub:
            ok = sub_idx >= d
            b3 = jnp.where(ok, a3 * pltpu.roll(b3, d, axis=1) + b3, b3)
            a3 = jnp.where(ok, a3 * pltpu.roll(a3, d, axis=1), a3)
            d *= 2
        a_sc[:, cs] = a3.reshape(rows, block)
        b_sc[:, cs] = b3.reshape(rows, block)

    def group(gi, h_prev):
        r0 = pl.multiple_of(gi * sub, sub)
        h8 = a_sc[pl.ds(r0, sub), :] * h_prev + b_sc[pl.ds(r0, sub), :]
        b_sc[pl.ds(r0, sub), :] = h8
        return jnp.broadcast_to(h8[sub - 1:sub, :], h8.shape)

    hlast_sc[...] = lax.fori_loop(0, rows // sub, group, hlast_sc[...])

    uy = uy_ref[...].astype(F32)
    gelu = 0.5 * uy * (1.0 + jnp.tanh(math.sqrt(2.0 / math.pi) * (uy + 0.044715 * (uy * uy * uy))))
    y = gelu * b_sc[...]
    o_ref[...] = (y * _rms_scale(y) * gout_ref[...]).astype(BF16)


def _rglru(proj, conv_w, conv_b, w_gates, b_rg, b_ig, lam, g_out, *, batch, seq, width, ux_col):
    nb, block, _ = w_gates.shape
    rows = _tile(seq, 256)
    nt = seq // rows
    ux_blk = ux_col // width
    vec = lambda: pl.BlockSpec((1, width), lambda b, t: (0, 0))
    kern = functools.partial(_lru_kernel, block=block)
    return pl.pallas_call(
        kern,
        grid=(batch, nt),
        in_specs=[
            pl.BlockSpec((rows, width), lambda b, t: (b * nt + t, ux_blk)),
            pl.BlockSpec((rows, width), lambda b, t: (b * nt + t, ux_blk + 1)),
            pl.BlockSpec((CONV_WIDTH, width), lambda b, t: (0, 0)),
            vec(),
            pl.BlockSpec((nb, block, 2 * block), lambda b, t: (0, 0, 0)),
            vec(), vec(), vec(), vec(),
        ],
        out_specs=pl.BlockSpec((rows, width), lambda b, t: (b * nt + t, 0)),
        out_shape=jax.ShapeDtypeStruct((batch * seq, width), BF16),
        scratch_shapes=[
            pltpu.VMEM((V7X_SUBLANES, width), F32),
            pltpu.VMEM((V7X_SUBLANES, width), F32),
            pltpu.VMEM((rows, width), F32),
            pltpu.VMEM((rows, width), F32),
            pltpu.VMEM((rows, width), F32),
        ],
        compiler_params=_params("parallel", "arbitrary"),
        name="rglru",
    )(proj, proj, conv_w, conv_b, w_gates, b_rg, b_ig, lam, g_out)


def _outproj_kernel(attn_ref, lru_ref, gattn_ref, w_ref, x_ref, h_ref, mixed_sc):
    @pl.when(pl.program_id(1) == 0)
    def _():
        a = attn_ref[...]
        wa = a.shape[1]
        mixed_sc[:, :wa] = (a * _rms_scale(a) * gattn_ref[...]).astype(BF16)
        mixed_sc[:, wa:] = lru_ref[...]

    h_ref[...] = x_ref[...] + jnp.dot(mixed_sc[...], w_ref[...], preferred_element_type=F32)


def _out_projection(attn, lru, g_attn, w_out, x2):
    m, wa = attn.shape
    wl = lru.shape[1]
    d = w_out.shape[1]
    tm = _tile(m, 512)
    tn = _tile(d, 1024)
    return pl.pallas_call(
        _outproj_kernel,
        grid=(m // tm, d // tn),
        in_specs=[
            pl.BlockSpec((tm, wa), lambda i, j: (i, 0)),
            pl.BlockSpec((tm, wl), lambda i, j: (i, 0)),
            pl.BlockSpec((1, wa), lambda i, j: (0, 0)),
            pl.BlockSpec((wa + wl, tn), lambda i, j: (0, j)),
            pl.BlockSpec((tm, tn), lambda i, j: (i, j)),
        ],
        out_specs=pl.BlockSpec((tm, tn), lambda i, j: (i, j)),
        out_shape=jax.ShapeDtypeStruct((m, d), F32),
        scratch_shapes=[pltpu.VMEM((tm, wa + wl), BF16)],
        compiler_params=_params("parallel", "arbitrary"),
        name="out_projection",
    )(attn, lru, g_attn, w_out, x2)


def _ffn_kernel(h_ref, gffn_ref, wg_ref, wu_ref, wd_ref, o_ref, hn_sc):
    @pl.when(pl.program_id(1) == 0)
    def _():
        h = h_ref[...]
        hn_sc[...] = (h * _rms_scale(h) * gffn_ref[...]).astype(BF16)
        o_ref[...] = h

    hn = hn_sc[...]
    g = jnp.dot(hn, wg_ref[...], preferred_element_type=F32)
    u = jnp.dot(hn, wu_ref[...], preferred_element_type=F32)
    act = (g * (1.0 / (1.0 + jnp.exp(-g))) * u).astype(BF16)
    o_ref[...] += jnp.dot(act, wd_ref[...], preferred_element_type=F32)


def _ffn(h, g_ffn, w_gate, w_up, w_down):
    m, d = h.shape
    dff = w_gate.shape[1]
    tm = _tile(m, 512)
    tf = _tile(dff, 256)
    return pl.pallas_call(
        _ffn_kernel,
        grid=(m // tm, dff // tf),
        in_specs=[
            pl.BlockSpec((tm, d), lambda i, f: (i, 0), pipeline_mode=pl.Buffered(1)),
            pl.BlockSpec((1, d), lambda i, f: (0, 0)),
            pl.BlockSpec((d, tf), lambda i, f: (0, f)),
            pl.BlockSpec((d, tf), lambda i, f: (0, f)),
            pl.BlockSpec((tf, d), lambda i, f: (f, 0)),
        ],
        out_specs=pl.BlockSpec((tm, d), lambda i, f: (i, 0)),
        out_shape=jax.ShapeDtypeStruct((m, d), F32),
        scratch_shapes=[pltpu.VMEM((tm, d), BF16)],
        compiler_params=_params("parallel", "arbitrary"),
        name="swiglu_ffn",
    )(h, g_ffn, w_gate, w_up, w_down)


def _layer(x, g_mix, w_in, b_f, g_q, g_k, conv_w, conv_b, w_rg, b_rg, w_ig, b_ig, lam,
           g_attn_out, g_lru_out, w_out, g_ffn, w_gate, w_up, w_down):
    batch, seq, d = x.shape
    n_heads = b_f.shape[0]
    head_dim = g_q.shape[0]
    attn_w = n_heads * head_dim
    lru_w = lam.shape[0]
    row = lambda v: v.reshape(1, -1).astype(F32)

    f_lo, f_hi = 3 * attn_w, 3 * attn_w + n_heads
    w_main = jnp.concatenate([w_in[:, :f_lo], w_in[:, f_hi:]], axis=1).astype(BF16)
    w_f = jnp.pad(w_in[:, f_lo:f_hi], ((0, 0), (0, V7X_LANES - n_heads))).astype(BF16)
    b_f_pad = jnp.pad(b_f.astype(F32), (0, V7X_LANES - n_heads)).reshape(1, V7X_LANES)
    qk_gain = jnp.concatenate([jnp.tile(g_q.astype(F32) * head_dim ** -0.5, n_heads),
                               jnp.tile(g_k.astype(F32), n_heads),
                               jnp.ones((w_main.shape[1] - 2 * attn_w,), F32)]).reshape(1, -1)
    w_gates = jnp.concatenate([w_rg, w_ig], axis=-1).astype(BF16)

    x2 = x.reshape(batch * seq, d)
    proj, f_logit = _in_projection(x2, row(g_mix), w_main, w_f, qk_gain,
                                   n_norm_cols=2 * attn_w, head_dim=head_dim)
    c_col = _forget_cumsum(f_logit.reshape(batch, seq, V7X_LANES), b_f_pad)
    c_row = c_col[:, :, :n_heads].transpose(0, 2, 1).reshape(batch * n_heads, 1, seq)
    attn = _attention(proj, c_col, c_row, batch=batch, seq=seq, n_heads=n_heads, head_dim=head_dim)
    lru = _rglru(proj, conv_w.astype(F32), row(conv_b), w_gates, row(b_rg), row(b_ig), row(lam),
                 row(g_lru_out), batch=batch, seq=seq, width=lru_w, ux_col=3 * attn_w)
    h = _out_projection(attn, lru, row(g_attn_out), w_out.astype(BF16), x2)
    out = _ffn(h, row(g_ffn), w_gate.astype(BF16), w_up.astype(BF16), w_down.astype(BF16))
    return out.reshape(batch, seq, d)


def kernel(x, g_mix, w_in, b_f, g_q, g_k, conv_w, conv_b, w_rg, b_rg, w_ig, b_ig, lam, g_attn_out, g_lru_out,
           w_out, g_ffn, w_gate, w_up, w_down):
    h = x
    for l in range(g_mix.shape[0]):
        h = _layer(h, g_mix[l], w_in[l], b_f[l], g_q[l], g_k[l], conv_w[l], conv_b[l], w_rg[l], b_rg[l],
                   w_ig[l], b_ig[l], lam[l], g_attn_out[l], g_lru_out[l], w_out[l], g_ffn[l], w_gate[l],
                   w_up[l], w_down[l])
    return h
```

```python
import functools
import math

import jax
import jax.numpy as jnp
from jax import lax
from jax.experimental import pallas as pl
from jax.experimental.pallas import tpu as pltpu

EPS = 1e-6
RG_C = 8.0
LOG2E = math.log2(math.e)
CONV_WIDTH = 4

V7X_LANES = 128
V7X_SUBLANES = 8
V7X_VMEM_LIMIT_BYTES = 56 * 1024 * 1024

F32 = jnp.float32
BF16 = jnp.bfloat16


def _tile(dim, pref):
    t = min(dim, pref)
    while dim % t:
        t //= 2
    return t


def _params(*sem):
    return pltpu.CompilerParams(dimension_semantics=sem, vmem_limit_bytes=V7X_VMEM_LIMIT_BYTES)


def _rms_scale(x):
    return lax.rsqrt(jnp.mean(x * x, axis=-1, keepdims=True) + EPS)


def _inproj_kernel(x_ref, gmix_ref, w_ref, wf_ref, gain_ref, proj_ref, f_ref, xn_sc, *, n_norm_tiles, head_dim):
    j = pl.program_id(1)

    @pl.when(j == 0)
    def _():
        x = x_ref[...]
        xn = (x * _rms_scale(x) * gmix_ref[...]).astype(BF16)
        xn_sc[...] = xn
        f_ref[...] = jnp.dot(xn, wf_ref[...], preferred_element_type=F32)

    y = jnp.dot(xn_sc[...], w_ref[...], preferred_element_type=F32)

    @pl.when(j < n_norm_tiles)
    def _():
        for c in range(0, y.shape[1], head_dim):
            ys = y[:, c:c + head_dim]
            proj_ref[:, c:c + head_dim] = (ys * _rms_scale(ys) * gain_ref[:, c:c + head_dim]).astype(BF16)

    @pl.when(j >= n_norm_tiles)
    def _():
        proj_ref[...] = y.astype(BF16)


def _in_projection(x2, g_mix, w_main, w_f, gain, *, n_norm_cols, head_dim):
    m, d = x2.shape
    n = w_main.shape[1]
    tm = _tile(m, 512)
    tn = _tile(math.gcd(n, n_norm_cols), 1024)
    kern = functools.partial(_inproj_kernel, n_norm_tiles=n_norm_cols // tn, head_dim=head_dim)
    return pl.pallas_call(
        kern,
        grid=(m // tm, n // tn),
        in_specs=[
            pl.BlockSpec((tm, d), lambda i, j: (i, 0)),
            pl.BlockSpec((1, d), lambda i, j: (0, 0)),
            pl.BlockSpec((d, tn), lambda i, j: (0, j)),
            pl.BlockSpec((d, V7X_LANES), lambda i, j: (0, 0)),
            pl.BlockSpec((1, tn), lambda i, j: (0, j)),
        ],
        out_specs=[
            pl.BlockSpec((tm, tn), lambda i, j: (i, j)),
            pl.BlockSpec((tm, V7X_LANES), lambda i, j: (i, 0)),
        ],
        out_shape=[
            jax.ShapeDtypeStruct((m, n), BF16),
            jax.ShapeDtypeStruct((m, V7X_LANES), F32),
        ],
        scratch_shapes=[pltpu.VMEM((tm, d), BF16)],
        compiler_params=_params("parallel", "arbitrary"),
        name="in_projection",
    )(x2, g_mix, w_main, w_f, gain)


def _forget_cumsum_kernel(f_ref, bf_ref, c_ref):
    z = f_ref[0] + bf_ref[...]
    c = jnp.minimum(z, 0.0) - jnp.log1p(jnp.exp(-jnp.abs(z)))
    row = lax.broadcasted_iota(jnp.int32, c.shape, 0)
    d = 1
    while d < c.shape[0]:
        c = c + jnp.where(row >= d, pltpu.roll(c, d, axis=0), 0.0)
        d *= 2
    c_ref[0] = c


def _forget_cumsum(f3, b_f_pad):
    b, s, lanes = f3.shape
    return pl.pallas_call(
        _forget_cumsum_kernel,
        grid=(b,),
        in_specs=[
            pl.BlockSpec((1, s, lanes), lambda i: (i, 0, 0)),
            pl.BlockSpec((1, lanes), lambda i: (0, 0)),
        ],
        out_specs=pl.BlockSpec((1, s, lanes), lambda i: (i, 0, 0)),
        out_shape=jax.ShapeDtypeStruct((b, s, lanes), F32),
        compiler_params=_params("parallel"),
        name="forget_cumsum",
    )(f3, b_f_pad)


def _bias_columns(c, ones_first):
    hi = c.astype(BF16).astype(F32)
    mid = (c - hi).astype(BF16).astype(F32)
    lo = (c - hi - mid).astype(BF16).astype(F32)
    lane = lax.broadcasted_iota(jnp.int32, (c.shape[0], V7X_LANES), 1)
    first_c, first_one = (3, 0) if ones_first else (0, 3)
    cols = jnp.where(lane == first_c, hi, jnp.where(lane == first_c + 1, mid, jnp.where(lane == first_c + 2, lo, 0.0)))
    ones = (lane >= first_one) & (lane < first_one + 3)
    return jnp.where(ones, 1.0, cols).astype(BF16)


def _attn_kernel(q_ref, k_ref, v_ref, c_ref, o_ref, qa_sc, ka_sc, va_sc, m_sc, acc_sc, s0_sc, s1_sc, *, tile, chunk):
    h = pl.program_id(1)
    qi = pl.program_id(2)
    seq, hd = k_ref.shape
    tq = 2 * tile
    lanes = m_sc.shape[1]

    def head_c(r0, n):
        cc = c_ref[0, pl.ds(r0, n), :]
        lane = lax.broadcasted_iota(jnp.int32, cc.shape, 1)
        return LOG2E * jnp.sum(jnp.where(lane == h, cc, 0.0), axis=1, keepdims=True)

    @pl.when(qi == 0)
    def _():
        def fill(ci, carry):
            r0 = pl.multiple_of(ci * chunk, chunk)
            ka_sc[pl.ds(r0, chunk), :hd] = k_ref[pl.ds(r0, chunk), :]
            ka_sc[pl.ds(r0, chunk), hd:] = _bias_columns(-head_c(r0, chunk), ones_first=True)
            va_sc[pl.ds(r0, chunk), :hd] = v_ref[pl.ds(r0, chunk), :]
            va_sc[pl.ds(r0, chunk), hd:] = jnp.ones((chunk, va_sc.shape[1] - hd), BF16)
            return carry
        lax.fori_loop(0, seq // chunk, fill, 0)

    qa_sc[:, :hd] = q_ref[...]
    qa_sc[:, hd:] = _bias_columns(head_c(pl.multiple_of(qi * tq, tq), tq), ones_first=False)
    m_sc[...] = jnp.full(m_sc.shape, -jnp.inf, F32)
    acc_sc[...] = jnp.zeros(acc_sc.shape, F32)

    def logits(sub, kb):
        k0 = pl.multiple_of(kb * tile, tile)
        return lax.dot_general(qa_sc[sub * tile:(sub + 1) * tile, :], ka_sc[pl.ds(k0, tile), :],
                               (((1,), (1,)), ((), ())), preferred_element_type=F32)

    def update(sub, s, kb, masked):
        rows = slice(sub * tile, (sub + 1) * tile)
        k0 = pl.multiple_of(kb * tile, tile)
        if masked:
            r = lax.broadcasted_iota(jnp.int32, s.shape, 0)
            c = lax.broadcasted_iota(jnp.int32, s.shape, 1)
            s = jnp.where(c <= r, s, -jnp.inf)
        m_prev = m_sc[rows, :]
        m_new = jnp.maximum(m_prev, jnp.max(s, axis=1, keepdims=True))
        alpha = jnp.exp2(m_prev - m_new)
        p = jnp.exp2(s - pltpu.repeat(m_new, tile // lanes, axis=1))
        pv = jnp.dot(p.astype(BF16), va_sc[pl.ds(k0, tile), :], preferred_element_type=F32)
        acc_sc[rows, :] = pltpu.repeat(alpha, acc_sc.shape[1] // lanes, axis=1) * acc_sc[rows, :] + pv
        m_sc[rows, :] = m_new

    for sub in range(2):
        s0_sc[sub] = logits(sub, 0)

    def block_pair(it, carry):
        kb = 2 * it
        for sub in range(2):
            s1_sc[sub] = logits(sub, kb + 1)
        for sub in range(2):
            update(sub, s0_sc[sub], kb, False)
        for sub in range(2):
            s0_sc[sub] = logits(sub, kb + 2)
        for sub in range(2):
            update(sub, s1_sc[sub], kb + 1, False)
        return carry

    lax.fori_loop(0, qi, block_pair, 0)
    kd = 2 * qi
    s_last = logits(1, kd + 1)
    update(0, s0_sc[0], kd, True)
    update(1, s0_sc[1], kd, False)
    update(1, s_last, kd + 1, True)
    o_ref[...] = acc_sc[:, :hd] / acc_sc[:, hd:2 * hd]


def _attention(proj, c_col, *, batch, seq, n_heads, head_dim):
    assert head_dim == V7X_LANES
    tile = _tile(seq, 512)
    tq = 2 * tile
    assert seq % tq == 0
    nq = seq // tq
    attn_w = n_heads * head_dim
    kern = functools.partial(_attn_kernel, tile=tile, chunk=tile)
    return pl.pallas_call(
        kern,
        grid=(batch, n_heads, nq),
        in_specs=[
            pl.BlockSpec((tq, head_dim), lambda b, h, i: (b * nq + i, h)),
            pl.BlockSpec((seq, head_dim), lambda b, h, i: (b, n_heads + h)),
            pl.BlockSpec((seq, head_dim), lambda b, h, i: (b, 2 * n_heads + h)),
            pl.BlockSpec((1, seq, V7X_LANES), lambda b, h, i: (b, 0, 0)),
        ],
        out_specs=pl.BlockSpec((tq, head_dim), lambda b, h, i: (b * nq + i, h)),
        out_shape=jax.ShapeDtypeStruct((batch * seq, attn_w), F32),
        scratch_shapes=[
            pltpu.VMEM((tq, 2 * head_dim), BF16),
            pltpu.VMEM((seq, 2 * head_dim), BF16),
            pltpu.VMEM((seq, 2 * head_dim), BF16),
            pltpu.VMEM((tq, V7X_LANES), F32),
            pltpu.VMEM((tq, 2 * head_dim), F32),
            pltpu.VMEM((2, tile, tile), F32),
            pltpu.VMEM((2, tile, tile), F32),
        ],
        compiler_params=_params("parallel", "parallel", "arbitrary"),
        name="fox_attention",
    )(proj, proj, proj, c_col)


def _lru_kernel(ux_ref, uy_ref, cw_ref, cb_ref, wg_ref, brg_ref, big_ref, lam_ref, gout_ref, o_ref,
                tail_sc, hlast_sc, xr_sc, a_sc, b_sc, *, block):
    t = pl.program_id(1)
    rows, width = xr_sc.shape
    sub = V7X_SUBLANES

    @pl.when(t == 0)
    def _():
        tail_sc[...] = jnp.zeros(tail_sc.shape, F32)
        hlast_sc[...] = jnp.zeros(hlast_sc.shape, F32)

    u = ux_ref[...].astype(F32)
    cw = cw_ref[...]
    cb = cb_ref[...]
    xr = cw[CONV_WIDTH - 1:CONV_WIDTH, :] * u + cb
    for d in range(1, CONV_WIDTH):
        xr = xr + cw[CONV_WIDTH - 1 - d:CONV_WIDTH - d, :] * pltpu.roll(u, d, axis=0)
    xr_sc[...] = xr
    u_head = u[0:sub, :]
    tail = tail_sc[...]
    row8 = lax.broadcasted_iota(jnp.int32, (sub, width), 0)
    xr_head = cw[CONV_WIDTH - 1:CONV_WIDTH, :] * u_head + cb
    for d in range(1, CONV_WIDTH):
        shifted = jnp.where(row8 < d, pltpu.roll(tail, d, axis=0), pltpu.roll(u_head, d, axis=0))
        xr_head = xr_head + cw[CONV_WIDTH - 1 - d:CONV_WIDTH - d, :] * shifted
    xr_sc[0:sub, :] = xr_head
    tail_sc[...] = u[rows - sub:rows, :]

    lam = lam_ref[...]
    neg_c_softplus = -RG_C * (jnp.maximum(-lam, 0.0) + jnp.log1p(jnp.exp(-jnp.abs(lam))))

    sub_idx = lax.broadcasted_iota(jnp.int32, (rows // sub, sub, block), 1)
    for n in range(width // block):
        cs = slice(n * block, (n + 1) * block)
        xb = xr_sc[:, cs]
        g = jnp.dot(xb.astype(BF16), wg_ref[n], preferred_element_type=F32)
        r = 1.0 / (1.0 + jnp.exp(-(g[:, :block] + brg_ref[:, cs])))
        i = 1.0 / (1.0 + jnp.exp(-(g[:, block:] + big_ref[:, cs])))
        log_a = r * neg_c_softplus[:, cs]
        a = jnp.exp(log_a)
        b = jnp.sqrt(1.0 - a * a) * (i * xb)
        a3 = a.reshape(rows // sub, sub, block)
        b3 = b.reshape(rows // sub, sub, block)
        d = 1
        while d < sub:
            ok = sub_idx >= d
            b3 = jnp.where(ok, a3 * pltpu.roll(b3, d, axis=1) + b3, b3)
            a3 = jnp.where(ok, a3 * pltpu.roll(a3, d, axis=1), a3)
            d *= 2
        a_sc[:, cs] = a3.reshape(rows, block)
        b_sc[:, cs] = b3.reshape(rows, block)

    def group(gi, h_prev):
        r0 = pl.multiple_of(gi * sub, sub)
        h8 = a_sc[pl.ds(r0, sub), :] * h_prev + b_sc[pl.ds(r0, sub), :]
        b_sc[pl.ds(r0, sub), :] = h8
        return jnp.broadcast_to(h8[sub - 1:sub, :], h8.shape)

    hlast_sc[...] = lax.fori_loop(0, rows // sub, group, hlast_sc[...])

    uy = uy_ref[...].astype(F32)
    gelu = 0.5 * uy * (1.0 + jnp.tanh(math.sqrt(2.0 / math.pi) * (uy + 0.044715 * (uy * uy * uy))))
    y = gelu * b_sc[...]
    o_ref[...] = (y * _rms_scale(y) * gout_ref[...]).astype(BF16)


def _rglru(proj, conv_w, conv_b, w_gates, b_rg, b_ig, lam, g_out, *, batch, seq, width, ux_col):
    nb, block, _ = w_gates.shape
    rows = _tile(seq, 256)
    nt = seq // rows
    ux_blk = ux_col // width
    vec = lambda: pl.BlockSpec((1, width), lambda b, t: (0, 0))
    kern = functools.partial(_lru_kernel, block=block)
    return pl.pallas_call(
        kern,
        grid=(batch, nt),
        in_specs=[
            pl.BlockSpec((rows, width), lambda b, t: (b * nt + t, ux_blk)),
            pl.BlockSpec((rows, width), lambda b, t: (b * nt + t, ux_blk + 1)),
            pl.BlockSpec((CONV_WIDTH, width), lambda b, t: (0, 0)),
            vec(),
            pl.BlockSpec((nb, block, 2 * block), lambda b, t: (0, 0, 0)),
            vec(), vec(), vec(), vec(),
        ],
        out_specs=pl.BlockSpec((rows, width), lambda b, t: (b * nt + t, 0)),
        out_shape=jax.ShapeDtypeStruct((batch * seq, width), BF16),
        scratch_shapes=[
            pltpu.VMEM((V7X_SUBLANES, width), F32),
            pltpu.VMEM((V7X_SUBLANES, width), F32),
            pltpu.VMEM((rows, width), F32),
            pltpu.VMEM((rows, width), F32),
            pltpu.VMEM((rows, width), F32),
        ],
        compiler_params=_params("parallel", "arbitrary"),
        name="rglru",
    )(proj, proj, conv_w, conv_b, w_gates, b_rg, b_ig, lam, g_out)


def _outproj_kernel(attn_ref, lru_ref, gattn_ref, w_ref, x_ref, h_ref, mixed_sc):
    @pl.when(pl.program_id(1) == 0)
    def _():
        a = attn_ref[...]
        wa = a.shape[1]
        mixed_sc[:, :wa] = (a * _rms_scale(a) * gattn_ref[...]).astype(BF16)
        mixed_sc[:, wa:] = lru_ref[...]

    h_ref[...] = x_ref[...] + jnp.dot(mixed_sc[...], w_ref[...], preferred_element_type=F32)


def _out_projection(attn, lru, g_attn, w_out, x2):
    m, wa = attn.shape
    wl = lru.shape[1]
    d = w_out.shape[1]
    tm = _tile(m, 512)
    tn = _tile(d, 1024)
    return pl.pallas_call(
        _outproj_kernel,
        grid=(m // tm, d // tn),
        in_specs=[
            pl.BlockSpec((tm, wa), lambda i, j: (i, 0)),
            pl.BlockSpec((tm, wl), lambda i, j: (i, 0)),
            pl.BlockSpec((1, wa), lambda i, j: (0, 0)),
            pl.BlockSpec((wa + wl, tn), lambda i, j: (0, j)),
            pl.BlockSpec((tm, tn), lambda i, j: (i, j)),
        ],
        out_specs=pl.BlockSpec((tm, tn), lambda i, j: (i, j)),
        out_shape=jax.ShapeDtypeStruct((m, d), F32),
        scratch_shapes=[pltpu.VMEM((tm, wa + wl), BF16)],
        compiler_params=_params("parallel", "arbitrary"),
        name="out_projection",
    )(attn, lru, g_attn, w_out, x2)


def _ffn_kernel(h_ref, gffn_ref, wg_ref, wu_ref, wd_ref, o_ref, hn_sc):
    @pl.when(pl.program_id(1) == 0)
    def _():
        h = h_ref[...]
        hn_sc[...] = (h * _rms_scale(h) * gffn_ref[...]).astype(BF16)
        o_ref[...] = h

    hn = hn_sc[...]
    g = jnp.dot(hn, wg_ref[...], preferred_element_type=F32)
    u = jnp.dot(hn, wu_ref[...], preferred_element_type=F32)
    act = (g * (1.0 / (1.0 + jnp.exp(-g))) * u).astype(BF16)
    o_ref[...] += jnp.dot(act, wd_ref[...], preferred_element_type=F32)


def _ffn(h, g_ffn, w_gate, w_up, w_down):
    m, d = h.shape
    dff = w_gate.shape[1]
    tm = _tile(m, 512)
    tf = _tile(dff, 256)
    return pl.pallas_call(
        _ffn_kernel,
        grid=(m // tm, dff // tf),
        in_specs=[
            pl.BlockSpec((tm, d), lambda i, f: (i, 0), pipeline_mode=pl.Buffered(1)),
            pl.BlockSpec((1, d), lambda i, f: (0, 0)),
            pl.BlockSpec((d, tf), lambda i, f: (0, f)),
            pl.BlockSpec((d, tf), lambda i, f: (0, f)),
            pl.BlockSpec((tf, d), lambda i, f: (f, 0)),
        ],
        out_specs=pl.BlockSpec((tm, d), lambda i, f: (i, 0)),
        out_shape=jax.ShapeDtypeStruct((m, d), F32),
        scratch_shapes=[pltpu.VMEM((tm, d), BF16)],
        compiler_params=_params("parallel", "arbitrary"),
        name="swiglu_ffn",
    )(h, g_ffn, w_gate, w_up, w_down)


def _layer(x, g_mix, w_in, b_f, g_q, g_k, conv_w, conv_b, w_rg, b_rg, w_ig, b_ig, lam,
           g_attn_out, g_lru_out, w_out, g_ffn, w_gate, w_up, w_down):
    batch, seq, d = x.shape
    n_heads = b_f.shape[0]
    head_dim = g_q.shape[0]
    attn_w = n_heads * head_dim
    lru_w = lam.shape[0]
    row = lambda v: v.reshape(1, -1).astype(F32)

    f_lo, f_hi = 3 * attn_w, 3 * attn_w + n_heads
    w_main = jnp.concatenate([w_in[:, :f_lo], w_in[:, f_hi:]], axis=1).astype(BF16)
    w_f = jnp.pad(w_in[:, f_lo:f_hi], ((0, 0), (0, V7X_LANES - n_heads))).astype(BF16)
    b_f_pad = jnp.pad(b_f.astype(F32), (0, V7X_LANES - n_heads)).reshape(1, V7X_LANES)
    qk_gain = jnp.concatenate([jnp.tile(g_q.astype(F32) * (head_dim ** -0.5 * LOG2E), n_heads),
                               jnp.tile(g_k.astype(F32), n_heads),
                               jnp.ones((w_main.shape[1] - 2 * attn_w,), F32)]).reshape(1, -1)
    w_gates = jnp.concatenate([w_rg, w_ig], axis=-1).astype(BF16)

    x2 = x.reshape(batch * seq, d)
    proj, f_logit = _in_projection(x2, row(g_mix), w_main, w_f, qk_gain,
                                   n_norm_cols=2 * attn_w, head_dim=head_dim)
    c_col = _forget_cumsum(f_logit.reshape(batch, seq, V7X_LANES), b_f_pad)
    attn = _attention(proj, c_col, batch=batch, seq=seq, n_heads=n_heads, head_dim=head_dim)
    lru = _rglru(proj, conv_w.astype(F32), row(conv_b), w_gates, row(b_rg), row(b_ig), row(lam),
                 row(g_lru_out), batch=batch, seq=seq, width=lru_w, ux_col=3 * attn_w)
    h = _out_projection(attn, lru, row(g_attn_out), w_out.astype(BF16), x2)
    out = _ffn(h, row(g_ffn), w_gate.astype(BF16), w_up.astype(BF16), w_down.astype(BF16))
    return out.reshape(batch, seq, d)


def kernel(x, g_mix, w_in, b_f, g_q, g_k, conv_w, conv_b, w_rg, b_rg, w_ig, b_ig, lam, g_attn_out, g_lru_out,
           w_out, g_ffn, w_gate, w_up, w_down):
    h = x
    for l in range(g_mix.shape[0]):
        h = _layer(h, g_mix[l], w_in[l], b_f[l], g_q[l], g_k[l], conv_w[l], conv_b[l], w_rg[l], b_rg[l],
                   w_ig[l], b_ig[l], lam[l], g_attn_out[l], g_lru_out[l], w_out[l], g_ffn[l], w_gate[l],
                   w_up[l], w_down[l])
    return h
```

```python
import functools
import math

import jax
import jax.numpy as jnp
from jax import lax
from jax.experimental import pallas as pl
from jax.experimental.pallas import tpu as pltpu

EPS = 1e-6
RG_C = 8.0
LOG2E = math.log2(math.e)
CONV_WIDTH = 4

V7X_LANES = 128
V7X_SUBLANES = 8
V7X_VMEM_LIMIT_BYTES = 56 * 1024 * 1024

F32 = jnp.float32
BF16 = jnp.bfloat16


def _tile(dim, pref):
    t = min(dim, pref)
    while dim % t:
        t //= 2
    return t


def _params(*sem):
    return pltpu.CompilerParams(dimension_semantics=sem, vmem_limit_bytes=V7X_VMEM_LIMIT_BYTES)


def _rms_scale(x):
    return lax.rsqrt(jnp.mean(x * x, axis=-1, keepdims=True) + EPS)


def _sigmoid(z):
    return 0.5 * jnp.tanh(0.5 * z) + 0.5


def _split_w_in_kernel(w_ref, main_ref, f_ref, *, f_lo, n_f):
    lanes = f_ref.shape[1]
    main_ref[:, :f_lo] = w_ref[:, :f_lo].astype(BF16)
    main_ref[:, f_lo:] = w_ref[:, pl.ds(f_lo + n_f, main_ref.shape[1] - f_lo)].astype(BF16)
    lane = lax.broadcasted_iota(jnp.int32, f_ref.shape, 1)
    f_ref[...] = jnp.where(lane < n_f, w_ref[:, f_lo:f_lo + lanes], 0.0).astype(BF16)


def _split_w_in(w_in, *, f_lo, n_f):
    d, n_in = w_in.shape
    assert f_lo % V7X_LANES == 0 and f_lo + V7X_LANES <= n_in
    tr = _tile(d, 128)
    return pl.pallas_call(
        functools.partial(_split_w_in_kernel, f_lo=f_lo, n_f=n_f),
        grid=(d // tr,),
        in_specs=[pl.BlockSpec((tr, n_in), lambda i: (i, 0))],
        out_specs=[pl.BlockSpec((tr, n_in - n_f), lambda i: (i, 0)),
                   pl.BlockSpec((tr, V7X_LANES), lambda i: (i, 0))],
        out_shape=[jax.ShapeDtypeStruct((d, n_in - n_f), BF16),
                   jax.ShapeDtypeStruct((d, V7X_LANES), BF16)],
        compiler_params=_params("parallel"),
        name="split_w_in",
    )(w_in)


def _inproj_kernel(x_ref, gmix_ref, w_ref, wf_ref, gain_ref, proj_ref, f_ref, xn_sc, *, n_norm_tiles, head_dim):
    j = pl.program_id(1)

    @pl.when(j == 0)
    def _():
        x = x_ref[...]
        xn = (x * _rms_scale(x) * gmix_ref[...]).astype(BF16)
        xn_sc[...] = xn
        f_ref[...] = jnp.dot(xn, wf_ref[...], preferred_element_type=F32)

    y = jnp.dot(xn_sc[...], w_ref[...], preferred_element_type=F32)

    @pl.when(j < n_norm_tiles)
    def _():
        for c in range(0, y.shape[1], head_dim):
            ys = y[:, c:c + head_dim]
            proj_ref[:, c:c + head_dim] = (ys * _rms_scale(ys) * gain_ref[:, c:c + head_dim]).astype(BF16)

    @pl.when(j >= n_norm_tiles)
    def _():
        proj_ref[...] = y.astype(BF16)


def _in_projection(x2, g_mix, w_main, w_f, gain, *, n_norm_cols, head_dim):
    m, d = x2.shape
    n = w_main.shape[1]
    tm = _tile(m, 512)
    tn = _tile(math.gcd(n, n_norm_cols), 1024)
    kern = functools.partial(_inproj_kernel, n_norm_tiles=n_norm_cols // tn, head_dim=head_dim)
    return pl.pallas_call(
        kern,
        grid=(m // tm, n // tn),
        in_specs=[
            pl.BlockSpec((tm, d), lambda i, j: (i, 0)),
            pl.BlockSpec((1, d), lambda i, j: (0, 0)),
            pl.BlockSpec((d, tn), lambda i, j: (0, j)),
            pl.BlockSpec((d, V7X_LANES), lambda i, j: (0, 0)),
            pl.BlockSpec((1, tn), lambda i, j: (0, j)),
        ],
        out_specs=[
            pl.BlockSpec((tm, tn), lambda i, j: (i, j)),
            pl.BlockSpec((tm, V7X_LANES), lambda i, j: (i, 0)),
        ],
        out_shape=[
            jax.ShapeDtypeStruct((m, n), BF16),
            jax.ShapeDtypeStruct((m, V7X_LANES), F32),
        ],
        scratch_shapes=[pltpu.VMEM((tm, d), BF16)],
        compiler_params=_params("parallel", "arbitrary"),
        name="in_projection",
    )(x2, g_mix, w_main, w_f, gain)


def _forget_cumsum_kernel(f_ref, bf_ref, c_ref):
    z = f_ref[0] + bf_ref[...]
    c = jnp.minimum(z, 0.0) - jnp.log1p(jnp.exp(-jnp.abs(z)))
    row = lax.broadcasted_iota(jnp.int32, c.shape, 0)
    d = 1
    while d < c.shape[0]:
        c = c + jnp.where(row >= d, pltpu.roll(c, d, axis=0), 0.0)
        d *= 2
    c_ref[0] = c


def _forget_cumsum(f3, b_f_pad):
    b, s, lanes = f3.shape
    return pl.pallas_call(
        _forget_cumsum_kernel,
        grid=(b,),
        in_specs=[
            pl.BlockSpec((1, s, lanes), lambda i: (i, 0, 0)),
            pl.BlockSpec((1, lanes), lambda i: (0, 0)),
        ],
        out_specs=pl.BlockSpec((1, s, lanes), lambda i: (i, 0, 0)),
        out_shape=jax.ShapeDtypeStruct((b, s, lanes), F32),
        compiler_params=_params("parallel"),
        name="forget_cumsum",
    )(f3, b_f_pad)


def _bias_columns(c, ones_first):
    hi = c.astype(BF16).astype(F32)
    mid = (c - hi).astype(BF16).astype(F32)
    lo = (c - hi - mid).astype(BF16).astype(F32)
    lane = lax.broadcasted_iota(jnp.int32, (c.shape[0], V7X_LANES), 1)
    first_c, first_one = (3, 0) if ones_first else (0, 3)
    cols = jnp.where(lane == first_c, hi, jnp.where(lane == first_c + 1, mid, jnp.where(lane == first_c + 2, lo, 0.0)))
    ones = (lane >= first_one) & (lane < first_one + 3)
    return jnp.where(ones, 1.0, cols).astype(BF16)


def _attn_kernel(q_ref, k_ref, v_ref, c_ref, o_ref, qa_sc, ka_sc, va_sc, m_sc, acc_sc, s0_sc, s1_sc, *, tile, chunk):
    h = pl.program_id(1)
    qi = pl.program_id(2)
    seq, hd = k_ref.shape
    tq = 2 * tile
    lanes = m_sc.shape[1]

    def head_c(r0, n):
        cc = c_ref[0, pl.ds(r0, n), :]
        lane = lax.broadcasted_iota(jnp.int32, cc.shape, 1)
        return LOG2E * jnp.sum(jnp.where(lane == h, cc, 0.0), axis=1, keepdims=True)

    @pl.when(qi == 0)
    def _():
        def fill(ci, carry):
            r0 = pl.multiple_of(ci * chunk, chunk)
            ka_sc[pl.ds(r0, chunk), :hd] = k_ref[pl.ds(r0, chunk), :]
            ka_sc[pl.ds(r0, chunk), hd:] = _bias_columns(-head_c(r0, chunk), ones_first=True)
            va_sc[pl.ds(r0, chunk), :hd] = v_ref[pl.ds(r0, chunk), :]
            va_sc[pl.ds(r0, chunk), hd:] = jnp.ones((chunk, va_sc.shape[1] - hd), BF16)
            return carry
        lax.fori_loop(0, seq // chunk, fill, 0)

    qa_sc[:, :hd] = q_ref[...]
    qa_sc[:, hd:] = _bias_columns(head_c(pl.multiple_of(qi * tq, tq), tq), ones_first=False)
    m_sc[...] = jnp.full(m_sc.shape, -jnp.inf, F32)
    acc_sc[...] = jnp.zeros(acc_sc.shape, F32)

    def logits(sub, kb):
        k0 = pl.multiple_of(kb * tile, tile)
        return lax.dot_general(qa_sc[sub * tile:(sub + 1) * tile, :], ka_sc[pl.ds(k0, tile), :],
                               (((1,), (1,)), ((), ())), preferred_element_type=F32)

    def update(sub, s, kb, masked):
        rows = slice(sub * tile, (sub + 1) * tile)
        k0 = pl.multiple_of(kb * tile, tile)
        if masked:
            r = lax.broadcasted_iota(jnp.int32, s.shape, 0)
            c = lax.broadcasted_iota(jnp.int32, s.shape, 1)
            s = jnp.where(c <= r, s, -jnp.inf)
        m_prev = m_sc[rows, :]
        m_new = jnp.maximum(m_prev, jnp.max(s, axis=1, keepdims=True))
        alpha = jnp.exp2(m_prev - m_new)
        p = jnp.exp2(s - jnp.tile(m_new, (1, tile // lanes)))
        pv = jnp.dot(p.astype(BF16), va_sc[pl.ds(k0, tile), :], preferred_element_type=F32)
        acc_sc[rows, :] = jnp.tile(alpha, (1, acc_sc.shape[1] // lanes)) * acc_sc[rows, :] + pv
        m_sc[rows, :] = m_new

    for sub in range(2):
        s0_sc[sub] = logits(sub, 0)

    def block_pair(kb):
        for sub in range(2):
            s1_sc[sub] = logits(sub, kb + 1)
        for sub in range(2):
            update(sub, s0_sc[sub], kb, False)
        for sub in range(2):
            s0_sc[sub] = logits(sub, kb + 2)
        for sub in range(2):
            update(sub, s1_sc[sub], kb + 1, False)

    def two_pairs(it, carry):
        block_pair(4 * it)
        block_pair(4 * it + 2)
        return carry

    def one_pair(it, carry):
        block_pair(4 * (qi // 2))
        return carry

    lax.fori_loop(0, qi // 2, two_pairs, 0)
    lax.fori_loop(0, qi % 2, one_pair, 0)
    kd = 2 * qi
    s_last = logits(1, kd + 1)
    update(0, s0_sc[0], kd, True)
    update(1, s0_sc[1], kd, False)
    update(1, s_last, kd + 1, True)
    o_ref[...] = acc_sc[:, :hd] / acc_sc[:, hd:2 * hd]


def _attention(proj, c_col, *, batch, seq, n_heads, head_dim):
    assert head_dim == V7X_LANES
    tile = _tile(seq, 512)
    tq = 2 * tile
    assert seq % tq == 0
    nq = seq // tq
    attn_w = n_heads * head_dim
    kern = functools.partial(_attn_kernel, tile=tile, chunk=tile)
    return pl.pallas_call(
        kern,
        grid=(batch, n_heads, nq),
        in_specs=[
            pl.BlockSpec((tq, head_dim), lambda b, h, i: (b * nq + i, h)),
            pl.BlockSpec((seq, head_dim), lambda b, h, i: (b, n_heads + h)),
            pl.BlockSpec((seq, head_dim), lambda b, h, i: (b, 2 * n_heads + h)),
            pl.BlockSpec((1, seq, V7X_LANES), lambda b, h, i: (b, 0, 0)),
        ],
        out_specs=pl.BlockSpec((tq, head_dim), lambda b, h, i: (b * nq + i, h)),
        out_shape=jax.ShapeDtypeStruct((batch * seq, attn_w), F32),
        scratch_shapes=[
            pltpu.VMEM((tq, 2 * head_dim), BF16),
            pltpu.VMEM((seq, 2 * head_dim), BF16),
            pltpu.VMEM((seq, 2 * head_dim), BF16),
            pltpu.VMEM((tq, V7X_LANES), F32),
            pltpu.VMEM((tq, 2 * head_dim), F32),
            pltpu.VMEM((2, tile, tile), F32),
            pltpu.VMEM((2, tile, tile), F32),
        ],
        compiler_params=_params("parallel", "parallel", "arbitrary"),
        name="fox_attention",
    )(proj, proj, proj, c_col)


def _lru_kernel(ux_ref, uy_ref, cw_ref, cb_ref, wg_ref, brg_ref, big_ref, lam_ref, gout_ref, o_ref,
                tail_sc, hlast_sc, xr_sc, a_sc, b_sc, *, block):
    t = pl.program_id(1)
    rows, width = xr_sc.shape
    sub = V7X_SUBLANES

    @pl.when(t == 0)
    def _():
        tail_sc[...] = jnp.zeros(tail_sc.shape, F32)
        hlast_sc[...] = jnp.zeros(hlast_sc.shape, F32)

    u = ux_ref[...].astype(F32)
    cw = cw_ref[...]
    cb = cb_ref[...]
    xr = cw[CONV_WIDTH - 1:CONV_WIDTH, :] * u + cb
    for d in range(1, CONV_WIDTH):
        xr = xr + cw[CONV_WIDTH - 1 - d:CONV_WIDTH - d, :] * pltpu.roll(u, d, axis=0)
    xr_sc[...] = xr
    u_head = u[0:sub, :]
    tail = tail_sc[...]
    row8 = lax.broadcasted_iota(jnp.int32, (sub, width), 0)
    xr_head = cw[CONV_WIDTH - 1:CONV_WIDTH, :] * u_head + cb
    for d in range(1, CONV_WIDTH):
        shifted = jnp.where(row8 < d, pltpu.roll(tail, d, axis=0), pltpu.roll(u_head, d, axis=0))
        xr_head = xr_head + cw[CONV_WIDTH - 1 - d:CONV_WIDTH - d, :] * shifted
    xr_sc[0:sub, :] = xr_head
    tail_sc[...] = u[rows - sub:rows, :]

    lam = lam_ref[...]
    log2_a_max = (-RG_C * LOG2E) * (jnp.maximum(-lam, 0.0) + jnp.log1p(jnp.exp(-jnp.abs(lam))))

    sub_idx = lax.broadcasted_iota(jnp.int32, (rows // sub, sub, block), 1)
    for n in range(width // block):
        cs = slice(n * block, (n + 1) * block)
        xb = xr_sc[:, cs]
        g = jnp.dot(xb.astype(BF16), wg_ref[n], preferred_element_type=F32)
        r = _sigmoid(g[:, :block] + brg_ref[:, cs])
        i = _sigmoid(g[:, block:] + big_ref[:, cs])
        a = jnp.exp2(r * log2_a_max[:, cs])
        b = jnp.sqrt(1.0 - a * a) * (i * xb)
        a3 = a.reshape(rows // sub, sub, block)
        b3 = b.reshape(rows // sub, sub, block)
        d = 1
        while d < sub:
            ok = sub_idx >= d
            b3 = jnp.where(ok, a3 * pltpu.roll(b3, d, axis=1) + b3, b3)
            a3 = jnp.where(ok, a3 * pltpu.roll(a3, d, axis=1), a3)
            d *= 2
        a_sc[:, cs] = a3.reshape(rows, block)
        b_sc[:, cs] = b3.reshape(rows, block)

    def group(gi, h_prev):
        r0 = pl.multiple_of(gi * sub, sub)
        h8 = a_sc[pl.ds(r0, sub), :] * h_prev + b_sc[pl.ds(r0, sub), :]
        b_sc[pl.ds(r0, sub), :] = h8
        return jnp.broadcast_to(h8[sub - 1:sub, :], h8.shape)

    hlast_sc[...] = lax.fori_loop(0, rows // sub, group, hlast_sc[...])

    uy = uy_ref[...].astype(F32)
    gelu = 0.5 * uy * (1.0 + jnp.tanh(math.sqrt(2.0 / math.pi) * (uy + 0.044715 * (uy * uy * uy))))
    y = gelu * b_sc[...]
    o_ref[...] = (y * _rms_scale(y) * gout_ref[...]).astype(BF16)


def _rglru(proj, conv_w, conv_b, w_gates, b_rg, b_ig, lam, g_out, *, batch, seq, width, ux_col):
    nb, block, _ = w_gates.shape
    rows = _tile(seq, 256)
    nt = seq // rows
    ux_blk = ux_col // width
    vec = lambda: pl.BlockSpec((1, width), lambda b, t: (0, 0))
    kern = functools.partial(_lru_kernel, block=block)
    return pl.pallas_call(
        kern,
        grid=(batch, nt),
        in_specs=[
            pl.BlockSpec((rows, width), lambda b, t: (b * nt + t, ux_blk)),
            pl.BlockSpec((rows, width), lambda b, t: (b * nt + t, ux_blk + 1)),
            pl.BlockSpec((CONV_WIDTH, width), lambda b, t: (0, 0)),
            vec(),
            pl.BlockSpec((nb, block, 2 * block), lambda b, t: (0, 0, 0)),
            vec(), vec(), vec(), vec(),
        ],
        out_specs=pl.BlockSpec((rows, width), lambda b, t: (b * nt + t, 0)),
        out_shape=jax.ShapeDtypeStruct((batch * seq, width), BF16),
        scratch_shapes=[
            pltpu.VMEM((V7X_SUBLANES, width), F32),
            pltpu.VMEM((V7X_SUBLANES, width), F32),
            pltpu.VMEM((rows, width), F32),
            pltpu.VMEM((rows, width), F32),
            pltpu.VMEM((rows, width), F32),
        ],
        compiler_params=_params("parallel", "arbitrary"),
        name="rglru",
    )(proj, proj, conv_w, conv_b, w_gates, b_rg, b_ig, lam, g_out)


def _outproj_kernel(attn_ref, lru_ref, gattn_ref, w_ref, x_ref, h_ref, mixed_sc):
    @pl.when(pl.program_id(1) == 0)
    def _():
        a = attn_ref[...]
        wa = a.shape[1]
        mixed_sc[:, :wa] = (a * _rms_scale(a) * gattn_ref[...]).astype(BF16)
        mixed_sc[:, wa:] = lru_ref[...]

    h_ref[...] = x_ref[...] + jnp.dot(mixed_sc[...], w_ref[...], preferred_element_type=F32)


def _out_projection(attn, lru, g_attn, w_out, x2):
    m, wa = attn.shape
    wl = lru.shape[1]
    d = w_out.shape[1]
    tm = _tile(m, 512)
    tn = _tile(d, 1024)
    return pl.pallas_call(
        _outproj_kernel,
        grid=(m // tm, d // tn),
        in_specs=[
            pl.BlockSpec((tm, wa), lambda i, j: (i, 0)),
            pl.BlockSpec((tm, wl), lambda i, j: (i, 0)),
            pl.BlockSpec((1, wa), lambda i, j: (0, 0)),
            pl.BlockSpec((wa + wl, tn), lambda i, j: (0, j)),
            pl.BlockSpec((tm, tn), lambda i, j: (i, j)),
        ],
        out_specs=pl.BlockSpec((tm, tn), lambda i, j: (i, j)),
        out_shape=jax.ShapeDtypeStruct((m, d), F32),
        scratch_shapes=[pltpu.VMEM((tm, wa + wl), BF16)],
        compiler_params=_params("parallel", "arbitrary"),
        name="out_projection",
    )(attn, lru, g_attn, w_out, x2)


def _ffn_kernel(h_ref, gffn_ref, wg_ref, wu_ref, wd_ref, o_ref, hn_sc):
    @pl.when(pl.program_id(1) == 0)
    def _():
        rows = _tile(h_ref.shape[0], 128)
        for r in range(0, h_ref.shape[0], rows):
            h = h_ref[r:r + rows, :]
            hn_sc[r:r + rows, :] = (h * _rms_scale(h) * gffn_ref[...]).astype(BF16)
            o_ref[r:r + rows, :] = h

    hn = hn_sc[...]
    g = jnp.dot(hn, wg_ref[...], preferred_element_type=F32)
    u = jnp.dot(hn, wu_ref[...], preferred_element_type=F32)
    act = (g * (1.0 / (1.0 + jnp.exp(-g))) * u).astype(BF16)
    chunk = _tile(o_ref.shape[1], 1024)
    for c in range(0, o_ref.shape[1], chunk):
        o_ref[:, c:c + chunk] += jnp.dot(act, wd_ref[:, c:c + chunk], preferred_element_type=F32)


def _ffn(h, g_ffn, w_gate, w_up, w_down):
    m, d = h.shape
    dff = w_gate.shape[1]
    tm = _tile(m, 512)
    tf = _tile(dff, 256)
    return pl.pallas_call(
        _ffn_kernel,
        grid=(m // tm, dff // tf),
        in_specs=[
            pl.BlockSpec((tm, d), lambda i, f: (i, 0)),
            pl.BlockSpec((1, d), lambda i, f: (0, 0)),
            pl.BlockSpec((d, tf), lambda i, f: (0, f)),
            pl.BlockSpec((d, tf), lambda i, f: (0, f)),
            pl.BlockSpec((tf, d), lambda i, f: (f, 0)),
        ],
        out_specs=pl.BlockSpec((tm, d), lambda i, f: (i, 0)),
        out_shape=jax.ShapeDtypeStruct((m, d), F32),
        scratch_shapes=[pltpu.VMEM((tm, d), BF16)],
        compiler_params=_params("parallel", "arbitrary"),
        name="swiglu_ffn",
    )(h, g_ffn, w_gate, w_up, w_down)


def _layer(x, g_mix, w_in, b_f, g_q, g_k, conv_w, conv_b, w_rg, b_rg, w_ig, b_ig, lam,
           g_attn_out, g_lru_out, w_out, g_ffn, w_gate, w_up, w_down):
    batch, seq, d = x.shape
    n_heads = b_f.shape[0]
    head_dim = g_q.shape[0]
    attn_w = n_heads * head_dim
    lru_w = lam.shape[0]
    row = lambda v: v.reshape(1, -1).astype(F32)

    f_lo, f_hi = 3 * attn_w, 3 * attn_w + n_heads
    w_main, w_f = _split_w_in(w_in, f_lo=f_lo, n_f=n_heads)
    b_f_pad = jnp.pad(b_f.astype(F32), (0, V7X_LANES - n_heads)).reshape(1, V7X_LANES)
    qk_gain = jnp.concatenate([jnp.tile(g_q.astype(F32) * (head_dim ** -0.5 * LOG2E), n_heads),
                               jnp.tile(g_k.astype(F32), n_heads),
                               jnp.ones((w_main.shape[1] - 2 * attn_w,), F32)]).reshape(1, -1)
    w_gates = jnp.concatenate([w_rg, w_ig], axis=-1).astype(BF16)

    x2 = x.reshape(batch * seq, d)
    proj, f_logit = _in_projection(x2, row(g_mix), w_main, w_f, qk_gain,
                                   n_norm_cols=2 * attn_w, head_dim=head_dim)
    c_col = _forget_cumsum(f_logit.reshape(batch, seq, V7X_LANES), b_f_pad)
    attn = _attention(proj, c_col, batch=batch, seq=seq, n_heads=n_heads, head_dim=head_dim)
    lru = _rglru(proj, conv_w.astype(F32), row(conv_b), w_gates, row(b_rg), row(b_ig), row(lam),
                 row(g_lru_out), batch=batch, seq=seq, width=lru_w, ux_col=3 * attn_w)
    h = _out_projection(attn, lru, row(g_attn_out), w_out.astype(BF16), x2)
    out = _ffn(h, row(g_ffn), w_gate.astype(BF16), w_up.astype(BF16), w_down.astype(BF16))
    return out.reshape(batch, seq, d)


def kernel(x, g_mix, w_in, b_f, g_q, g_k, conv_w, conv_b, w_rg, b_rg, w_ig, b_ig, lam, g_attn_out, g_lru_out,
           w_out, g_ffn, w_gate, w_up, w_down):
    h = x
    for l in range(g_mix.shape[0]):
        h = _layer(h, g_mix[l], w_in[l], b_f[l], g_q[l], g_k[l], conv_w[l], conv_b[l], w_rg[l], b_rg[l],
                   w_ig[l], b_ig[l], lam[l], g_attn_out[l], g_lru_out[l], w_out[l], g_ffn[l], w_gate[l],
                   w_up[l], w_down[l])
    return h
```

```python
import functools
import math

import jax
import jax.numpy as jnp
from jax import lax
from jax.experimental import pallas as pl
from jax.experimental.pallas import tpu as pltpu

EPS = 1e-6
RG_C = 8.0
LOG2E = math.log2(math.e)
CONV_WIDTH = 4

V7X_LANES = 128
V7X_SUBLANES = 8
V7X_VMEM_LIMIT_BYTES = 56 * 1024 * 1024

F32 = jnp.float32
BF16 = jnp.bfloat16


def _tile(dim, pref):
    t = min(dim, pref)
    while dim % t:
        t //= 2
    return t


def _params(*sem):
    return pltpu.CompilerParams(dimension_semantics=sem, vmem_limit_bytes=V7X_VMEM_LIMIT_BYTES)


def _rms_scale(x):
    return lax.rsqrt(jnp.mean(x * x, axis=-1, keepdims=True) + EPS)


def _sigmoid(z):
    return 0.5 * jnp.tanh(0.5 * z) + 0.5


def _split_w_in_kernel(wt_ref, ft_ref, main_ref, f_ref, *, n_f):
    main_ref[...] = wt_ref[...].T.astype(BF16)

    @pl.when(pl.program_id(0) == 0)
    def _():
        lane = lax.broadcasted_iota(jnp.int32, f_ref.shape, 1)
        f_ref[...] = jnp.where(lane < n_f, ft_ref[...].T, 0.0).astype(BF16)


def _split_w_in(w_in_t, *, f_lo, n_f):
    n_in, d = w_in_t.shape
    n_main = n_in - n_f
    cols = _tile(math.gcd(f_lo, n_main - f_lo), 256)
    assert cols % V7X_LANES == 0 and f_lo + V7X_LANES <= n_in and n_f % V7X_SUBLANES == 0
    return pl.pallas_call(
        functools.partial(_split_w_in_kernel, n_f=n_f),
        grid=(n_main // cols,),
        in_specs=[
            pl.BlockSpec((pl.Element(cols), pl.Element(d)),
                         lambda j: (pl.multiple_of(jnp.where(j * cols < f_lo, j * cols, j * cols + n_f),
                                                   V7X_SUBLANES), 0)),
            pl.BlockSpec((pl.Element(V7X_LANES), pl.Element(d)), lambda j: (f_lo, 0)),
        ],
        out_specs=[pl.BlockSpec((d, cols), lambda j: (0, j)),
                   pl.BlockSpec((d, V7X_LANES), lambda j: (0, 0))],
        out_shape=[jax.ShapeDtypeStruct((d, n_main), BF16),
                   jax.ShapeDtypeStruct((d, V7X_LANES), BF16)],
        compiler_params=_params("arbitrary"),
        name="split_w_in",
    )(w_in_t, w_in_t)


def _inproj_kernel(x_ref, gmix_ref, w_ref, wf_ref, gain_ref, proj_ref, f_ref, xn_sc, *, n_norm_tiles, head_dim):
    j = pl.program_id(1)

    @pl.when(j == 0)
    def _():
        x = x_ref[...]
        xn = (x * _rms_scale(x) * gmix_ref[...]).astype(BF16)
        xn_sc[...] = xn
        f_ref[...] = jnp.dot(xn, wf_ref[...], preferred_element_type=F32)

    y = jnp.dot(xn_sc[...], w_ref[...], preferred_element_type=F32)

    @pl.when(j < n_norm_tiles)
    def _():
        for c in range(0, y.shape[1], head_dim):
            ys = y[:, c:c + head_dim]
            proj_ref[:, c:c + head_dim] = (ys * _rms_scale(ys) * gain_ref[:, c:c + head_dim]).astype(BF16)

    @pl.when(j >= n_norm_tiles)
    def _():
        proj_ref[...] = y.astype(BF16)


def _in_projection(x2, g_mix, w_main, w_f, gain, *, n_norm_cols, head_dim):
    m, d = x2.shape
    n = w_main.shape[1]
    tm = _tile(m, 512)
    tn = _tile(math.gcd(n, n_norm_cols), 1024)
    kern = functools.partial(_inproj_kernel, n_norm_tiles=n_norm_cols // tn, head_dim=head_dim)
    return pl.pallas_call(
        kern,
        grid=(m // tm, n // tn),
        in_specs=[
            pl.BlockSpec((tm, d), lambda i, j: (i, 0)),
            pl.BlockSpec((1, d), lambda i, j: (0, 0)),
            pl.BlockSpec((d, tn), lambda i, j: (0, j)),
            pl.BlockSpec((d, V7X_LANES), lambda i, j: (0, 0)),
            pl.BlockSpec((1, tn), lambda i, j: (0, j)),
        ],
        out_specs=[
            pl.BlockSpec((tm, tn), lambda i, j: (i, j)),
            pl.BlockSpec((tm, V7X_LANES), lambda i, j: (i, 0)),
        ],
        out_shape=[
            jax.ShapeDtypeStruct((m, n), BF16),
            jax.ShapeDtypeStruct((m, V7X_LANES), F32),
        ],
        scratch_shapes=[pltpu.VMEM((tm, d), BF16)],
        compiler_params=_params("parallel", "arbitrary"),
        name="in_projection",
    )(x2, g_mix, w_main, w_f, gain)


def _forget_cumsum_kernel(f_ref, bf_ref, c_ref):
    z = f_ref[0] + bf_ref[...]
    c = jnp.minimum(z, 0.0) - jnp.log1p(jnp.exp(-jnp.abs(z)))
    row = lax.broadcasted_iota(jnp.int32, c.shape, 0)
    d = 1
    while d < c.shape[0]:
        c = c + jnp.where(row >= d, pltpu.roll(c, d, axis=0), 0.0)
        d *= 2
    c_ref[0] = c


def _forget_cumsum(f3, b_f_pad):
    b, s, lanes = f3.shape
    return pl.pallas_call(
        _forget_cumsum_kernel,
        grid=(b,),
        in_specs=[
            pl.BlockSpec((1, s, lanes), lambda i: (i, 0, 0)),
            pl.BlockSpec((1, lanes), lambda i: (0, 0)),
        ],
        out_specs=pl.BlockSpec((1, s, lanes), lambda i: (i, 0, 0)),
        out_shape=jax.ShapeDtypeStruct((b, s, lanes), F32),
        compiler_params=_params("parallel"),
        name="forget_cumsum",
    )(f3, b_f_pad)


def _bias_columns(c, ones_first):
    hi = c.astype(BF16).astype(F32)
    mid = (c - hi).astype(BF16).astype(F32)
    lo = (c - hi - mid).astype(BF16).astype(F32)
    lane = lax.broadcasted_iota(jnp.int32, (c.shape[0], V7X_LANES), 1)
    first_c, first_one = (3, 0) if ones_first else (0, 3)
    cols = jnp.where(lane == first_c, hi, jnp.where(lane == first_c + 1, mid, jnp.where(lane == first_c + 2, lo, 0.0)))
    ones = (lane >= first_one) & (lane < first_one + 3)
    return jnp.where(ones, 1.0, cols).astype(BF16)


def _attn_kernel(*refs, tile, chunk, n_cast):
    q_ref, k_ref, v_ref, c_ref = refs[:4]
    w32_refs = refs[4:4 + n_cast]
    o_ref = refs[4 + n_cast]
    w16_refs = refs[5 + n_cast:5 + 2 * n_cast]
    qa_sc, ka_sc, va_sc, m_sc, acc_sc, s0_sc, s1_sc = refs[5 + 2 * n_cast:]
    for w32_ref, w16_ref in zip(w32_refs, w16_refs):
        w16_ref[...] = w32_ref[...].astype(BF16)

    h = pl.program_id(1)
    qi = pl.program_id(2)
    seq, hd = k_ref.shape
    tq = 2 * tile
    lanes = m_sc.shape[1]

    def head_c(r0, n):
        cc = c_ref[0, pl.ds(r0, n), :]
        lane = lax.broadcasted_iota(jnp.int32, cc.shape, 1)
        return LOG2E * jnp.sum(jnp.where(lane == h, cc, 0.0), axis=1, keepdims=True)

    @pl.when(qi == 0)
    def _():
        def fill(ci, carry):
            r0 = pl.multiple_of(ci * chunk, chunk)
            ka_sc[pl.ds(r0, chunk), :hd] = k_ref[pl.ds(r0, chunk), :]
            ka_sc[pl.ds(r0, chunk), hd:] = _bias_columns(-head_c(r0, chunk), ones_first=True)
            va_sc[pl.ds(r0, chunk), :hd] = v_ref[pl.ds(r0, chunk), :]
            va_sc[pl.ds(r0, chunk), hd:] = jnp.ones((chunk, va_sc.shape[1] - hd), BF16)
            return carry
        lax.fori_loop(0, seq // chunk, fill, 0)

    qa_sc[:, :hd] = q_ref[...]
    qa_sc[:, hd:] = _bias_columns(head_c(pl.multiple_of(qi * tq, tq), tq), ones_first=False)
    m_sc[...] = jnp.full(m_sc.shape, -jnp.inf, F32)
    acc_sc[...] = jnp.zeros(acc_sc.shape, F32)

    def logits(sub, kb):
        k0 = pl.multiple_of(kb * tile, tile)
        return lax.dot_general(qa_sc[sub * tile:(sub + 1) * tile, :], ka_sc[pl.ds(k0, tile), :],
                               (((1,), (1,)), ((), ())), preferred_element_type=F32)

    def update(sub, s, kb, masked):
        rows = slice(sub * tile, (sub + 1) * tile)
        k0 = pl.multiple_of(kb * tile, tile)
        if masked:
            r = lax.broadcasted_iota(jnp.int32, s.shape, 0)
            c = lax.broadcasted_iota(jnp.int32, s.shape, 1)
            s = jnp.where(c <= r, s, -jnp.inf)
        m_prev = m_sc[rows, :]
        m_new = jnp.maximum(m_prev, jnp.max(s, axis=1, keepdims=True))
        alpha = jnp.exp2(m_prev - m_new)
        p = jnp.exp2(s - jnp.tile(m_new, (1, tile // lanes)))
        pv = jnp.dot(p.astype(BF16), va_sc[pl.ds(k0, tile), :], preferred_element_type=F32)
        acc_sc[rows, :] = jnp.tile(alpha, (1, acc_sc.shape[1] // lanes)) * acc_sc[rows, :] + pv
        m_sc[rows, :] = m_new

    for sub in range(2):
        s0_sc[sub] = logits(sub, 0)

    def block_pair(kb):
        for sub in range(2):
            s1_sc[sub] = logits(sub, kb + 1)
        for sub in range(2):
            update(sub, s0_sc[sub], kb, False)
        for sub in range(2):
            s0_sc[sub] = logits(sub, kb + 2)
        for sub in range(2):
            update(sub, s1_sc[sub], kb + 1, False)

    def two_pairs(it, carry):
        block_pair(4 * it)
        block_pair(4 * it + 2)
        return carry

    def one_pair(it, carry):
        block_pair(4 * (qi // 2))
        return carry

    lax.fori_loop(0, qi // 2, two_pairs, 0)
    lax.fori_loop(0, qi % 2, one_pair, 0)
    kd = 2 * qi
    s_last = logits(1, kd + 1)
    update(0, s0_sc[0], kd, True)
    update(1, s0_sc[1], kd, False)
    update(1, s_last, kd + 1, True)
    o_ref[...] = acc_sc[:, :hd] / acc_sc[:, hd:2 * hd]


def _cast_block_spec(rows, cols, n_steps, step_of):
    bf16_rows = 2 * V7X_SUBLANES
    br = next(r for r in range(bf16_rows, rows + 1, bf16_rows) if rows % r == 0 and rows // r <= n_steps)
    last = rows // br - 1
    return pl.BlockSpec((br, cols), lambda b, h, i: (jnp.minimum(step_of(b, h, i), last), 0))


def _attention(proj, c_col, weights_f32, *, batch, seq, n_heads, head_dim):
    assert head_dim == V7X_LANES
    tile = _tile(seq, 512)
    tq = 2 * tile
    assert seq % tq == 0
    nq = seq // tq
    attn_w = n_heads * head_dim
    n_steps = batch * n_heads * nq
    step_of = lambda b, h, i: (b * n_heads + h) * nq + i
    cast_specs = [_cast_block_spec(*w.shape, n_steps, step_of) for w in weights_f32]
    kern = functools.partial(_attn_kernel, tile=tile, chunk=tile, n_cast=len(weights_f32))
    return pl.pallas_call(
        kern,
        grid=(batch, n_heads, nq),
        in_specs=[
            pl.BlockSpec((tq, head_dim), lambda b, h, i: (b * nq + i, h)),
            pl.BlockSpec((seq, head_dim), lambda b, h, i: (b, n_heads + h)),
            pl.BlockSpec((seq, head_dim), lambda b, h, i: (b, 2 * n_heads + h)),
            pl.BlockSpec((1, seq, V7X_LANES), lambda b, h, i: (b, 0, 0)),
        ] + cast_specs,
        out_specs=[pl.BlockSpec((tq, head_dim), lambda b, h, i: (b * nq + i, h))] + cast_specs,
        out_shape=[jax.ShapeDtypeStruct((batch * seq, attn_w), F32)]
        + [jax.ShapeDtypeStruct(w.shape, BF16) for w in weights_f32],
        scratch_shapes=[
            pltpu.VMEM((tq, 2 * head_dim), BF16),
            pltpu.VMEM((seq, 2 * head_dim), BF16),
            pltpu.VMEM((seq, 2 * head_dim), BF16),
            pltpu.VMEM((tq, V7X_LANES), F32),
            pltpu.VMEM((tq, 2 * head_dim), F32),
            pltpu.VMEM((2, tile, tile), F32),
            pltpu.VMEM((2, tile, tile), F32),
        ],
        compiler_params=_params("parallel", "parallel", "arbitrary"),
        name="fox_attention",
    )(proj, proj, proj, c_col, *weights_f32)


def _lru_kernel(ux_ref, uy_ref, cw_ref, cb_ref, wg_ref, brg_ref, big_ref, lam_ref, gout_ref, o_ref,
                tail_sc, hlast_sc, xr_sc, a_sc, b_sc, *, block):
    t = pl.program_id(1)
    rows, width = xr_sc.shape
    sub = V7X_SUBLANES

    @pl.when(t == 0)
    def _():
        tail_sc[...] = jnp.zeros(tail_sc.shape, F32)
        hlast_sc[...] = jnp.zeros(hlast_sc.shape, F32)

    u = ux_ref[...].astype(F32)
    cw = cw_ref[...]
    cb = cb_ref[...]
    xr = cw[CONV_WIDTH - 1:CONV_WIDTH, :] * u + cb
    for d in range(1, CONV_WIDTH):
        xr = xr + cw[CONV_WIDTH - 1 - d:CONV_WIDTH - d, :] * pltpu.roll(u, d, axis=0)
    xr_sc[...] = xr
    u_head = u[0:sub, :]
    tail = tail_sc[...]
    row8 = lax.broadcasted_iota(jnp.int32, (sub, width), 0)
    xr_head = cw[CONV_WIDTH - 1:CONV_WIDTH, :] * u_head + cb
    for d in range(1, CONV_WIDTH):
        shifted = jnp.where(row8 < d, pltpu.roll(tail, d, axis=0), pltpu.roll(u_head, d, axis=0))
        xr_head = xr_head + cw[CONV_WIDTH - 1 - d:CONV_WIDTH - d, :] * shifted
    xr_sc[0:sub, :] = xr_head
    tail_sc[...] = u[rows - sub:rows, :]

    lam = lam_ref[...]
    log2_a_max = (-RG_C * LOG2E) * (jnp.maximum(-lam, 0.0) + jnp.log1p(jnp.exp(-jnp.abs(lam))))

    sub_idx = lax.broadcasted_iota(jnp.int32, (rows // sub, sub, block), 1)
    for n in range(width // block):
        cs = slice(n * block, (n + 1) * block)
        xb = xr_sc[:, cs]
        g = jnp.dot(xb.astype(BF16), wg_ref[n], preferred_element_type=F32)
        r = _sigmoid(g[:, :block] + brg_ref[:, cs])
        i = _sigmoid(g[:, block:] + big_ref[:, cs])
        a = jnp.exp2(r * log2_a_max[:, cs])
        b = jnp.sqrt(1.0 - a * a) * (i * xb)
        a3 = a.reshape(rows // sub, sub, block)
        b3 = b.reshape(rows // sub, sub, block)
        d = 1
        while d < sub:
            ok = sub_idx >= d
            b3 = jnp.where(ok, a3 * pltpu.roll(b3, d, axis=1) + b3, b3)
            a3 = jnp.where(ok, a3 * pltpu.roll(a3, d, axis=1), a3)
            d *= 2
        a_sc[:, cs] = a3.reshape(rows, block)
        b_sc[:, cs] = b3.reshape(rows, block)

    def group(gi, h_prev):
        r0 = pl.multiple_of(gi * sub, sub)
        h8 = a_sc[pl.ds(r0, sub), :] * h_prev + b_sc[pl.ds(r0, sub), :]
        b_sc[pl.ds(r0, sub), :] = h8
        return jnp.broadcast_to(h8[sub - 1:sub, :], h8.shape)

    hlast_sc[...] = lax.fori_loop(0, rows // sub, group, hlast_sc[...])

    uy = uy_ref[...].astype(F32)
    gelu = 0.5 * uy * (1.0 + jnp.tanh(math.sqrt(2.0 / math.pi) * (uy + 0.044715 * (uy * uy * uy))))
    y = gelu * b_sc[...]
    o_ref[...] = (y * _rms_scale(y) * gout_ref[...]).astype(BF16)


def _rglru(proj, conv_w, conv_b, w_gates, b_rg, b_ig, lam, g_out, *, batch, seq, width, ux_col):
    nb, block, _ = w_gates.shape
    rows = _tile(seq, 256)
    nt = seq // rows
    ux_blk = ux_col // width
    vec = lambda: pl.BlockSpec((1, width), lambda b, t: (0, 0))
    kern = functools.partial(_lru_kernel, block=block)
    return pl.pallas_call(
        kern,
        grid=(batch, nt),
        in_specs=[
            pl.BlockSpec((rows, width), lambda b, t: (b * nt + t, ux_blk)),
            pl.BlockSpec((rows, width), lambda b, t: (b * nt + t, ux_blk + 1)),
            pl.BlockSpec((CONV_WIDTH, width), lambda b, t: (0, 0)),
            vec(),
            pl.BlockSpec((nb, block, 2 * block), lambda b, t: (0, 0, 0)),
            vec(), vec(), vec(), vec(),
        ],
        out_specs=pl.BlockSpec((rows, width), lambda b, t: (b * nt + t, 0)),
        out_shape=jax.ShapeDtypeStruct((batch * seq, width), BF16),
        scratch_shapes=[
            pltpu.VMEM((V7X_SUBLANES, width), F32),
            pltpu.VMEM((V7X_SUBLANES, width), F32),
            pltpu.VMEM((rows, width), F32),
            pltpu.VMEM((rows, width), F32),
            pltpu.VMEM((rows, width), F32),
        ],
        compiler_params=_params("parallel", "arbitrary"),
        name="rglru",
    )(proj, proj, conv_w, conv_b, w_gates, b_rg, b_ig, lam, g_out)


def _outproj_kernel(attn_ref, lru_ref, gattn_ref, w_ref, x_ref, h_ref, mixed_sc):
    @pl.when(pl.program_id(1) == 0)
    def _():
        a = attn_ref[...]
        wa = a.shape[1]
        mixed_sc[:, :wa] = (a * _rms_scale(a) * gattn_ref[...]).astype(BF16)
        mixed_sc[:, wa:] = lru_ref[...]

    h_ref[...] = x_ref[...] + jnp.dot(mixed_sc[...], w_ref[...], preferred_element_type=F32)


def _out_projection(attn, lru, g_attn, w_out, x2):
    m, wa = attn.shape
    wl = lru.shape[1]
    d = w_out.shape[1]
    tm = _tile(m, 512)
    tn = _tile(d, 1024)
    return pl.pallas_call(
        _outproj_kernel,
        grid=(m // tm, d // tn),
        in_specs=[
            pl.BlockSpec((tm, wa), lambda i, j: (i, 0)),
            pl.BlockSpec((tm, wl), lambda i, j: (i, 0)),
            pl.BlockSpec((1, wa), lambda i, j: (0, 0)),
            pl.BlockSpec((wa + wl, tn), lambda i, j: (0, j)),
            pl.BlockSpec((tm, tn), lambda i, j: (i, j)),
        ],
        out_specs=pl.BlockSpec((tm, tn), lambda i, j: (i, j)),
        out_shape=jax.ShapeDtypeStruct((m, d), F32),
        scratch_shapes=[pltpu.VMEM((tm, wa + wl), BF16)],
        compiler_params=_params("parallel", "arbitrary"),
        name="out_projection",
    )(attn, lru, g_attn, w_out, x2)


def _ffn_kernel(h_ref, gffn_ref, wg_ref, wu_ref, wd_ref, o_ref, hn_sc):
    @pl.when(pl.program_id(1) == 0)
    def _():
        rows = _tile(h_ref.shape[0], 128)
        for r in range(0, h_ref.shape[0], rows):
            h = h_ref[r:r + rows, :]
            hn_sc[r:r + rows, :] = (h * _rms_scale(h) * gffn_ref[...]).astype(BF16)
            o_ref[r:r + rows, :] = h

    hn = hn_sc[...]
    g = jnp.dot(hn, wg_ref[...], preferred_element_type=F32)
    u = jnp.dot(hn, wu_ref[...], preferred_element_type=F32)
    act = (g * (1.0 / (1.0 + jnp.exp(-g))) * u).astype(BF16)
    chunk = _tile(o_ref.shape[1], 1024)
    for c in range(0, o_ref.shape[1], chunk):
        o_ref[:, c:c + chunk] += jnp.dot(act, wd_ref[:, c:c + chunk], preferred_element_type=F32)


def _ffn(h, g_ffn, w_gate, w_up, w_down):
    m, d = h.shape
    dff = w_gate.shape[1]
    tm = _tile(m, 512)
    tf = _tile(dff, 256)
    return pl.pallas_call(
        _ffn_kernel,
        grid=(m // tm, dff // tf),
        in_specs=[
            pl.BlockSpec((tm, d), lambda i, f: (i, 0)),
            pl.BlockSpec((1, d), lambda i, f: (0, 0)),
            pl.BlockSpec((d, tf), lambda i, f: (0, f)),
            pl.BlockSpec((d, tf), lambda i, f: (0, f)),
            pl.BlockSpec((tf, d), lambda i, f: (f, 0)),
        ],
        out_specs=pl.BlockSpec((tm, d), lambda i, f: (i, 0)),
        out_shape=jax.ShapeDtypeStruct((m, d), F32),
        scratch_shapes=[pltpu.VMEM((tm, d), BF16)],
        compiler_params=_params("parallel", "arbitrary"),
        name="swiglu_ffn",
    )(h, g_ffn, w_gate, w_up, w_down)


def _layer(x, g_mix, w_in, b_f, g_q, g_k, conv_w, conv_b, w_rg, b_rg, w_ig, b_ig, lam,
           g_attn_out, g_lru_out, w_out, g_ffn, w_gate, w_up, w_down):
    batch, seq, d = x.shape
    n_heads = b_f.shape[0]
    head_dim = g_q.shape[0]
    attn_w = n_heads * head_dim
    lru_w = lam.shape[0]
    row = lambda v: v.reshape(1, -1).astype(F32)

    f_lo, f_hi = 3 * attn_w, 3 * attn_w + n_heads
    w_main, w_f = _split_w_in(jnp.swapaxes(w_in, 0, 1), f_lo=f_lo, n_f=n_heads)
    b_f_pad = jnp.pad(b_f.astype(F32), (0, V7X_LANES - n_heads)).reshape(1, V7X_LANES)
    qk_gain = jnp.concatenate([jnp.tile(g_q.astype(F32) * (head_dim ** -0.5 * LOG2E), n_heads),
                               jnp.tile(g_k.astype(F32), n_heads),
                               jnp.ones((w_main.shape[1] - 2 * attn_w,), F32)]).reshape(1, -1)
    w_gates = jnp.concatenate([w_rg, w_ig], axis=-1).astype(BF16)

    x2 = x.reshape(batch * seq, d)
    proj, f_logit = _in_projection(x2, row(g_mix), w_main, w_f, qk_gain,
                                   n_norm_cols=2 * attn_w, head_dim=head_dim)
    c_col = _forget_cumsum(f_logit.reshape(batch, seq, V7X_LANES), b_f_pad)
    attn, w_out16, w_gate16, w_up16, w_down16 = _attention(
        proj, c_col, [w.astype(F32) for w in (w_out, w_gate, w_up, w_down)],
        batch=batch, seq=seq, n_heads=n_heads, head_dim=head_dim)
    lru = _rglru(proj, conv_w.astype(F32), row(conv_b), w_gates, row(b_rg), row(b_ig), row(lam),
                 row(g_lru_out), batch=batch, seq=seq, width=lru_w, ux_col=3 * attn_w)
    h = _out_projection(attn, lru, row(g_attn_out), w_out16, x2)
    out = _ffn(h, row(g_ffn), w_gate16, w_up16, w_down16)
    return out.reshape(batch, seq, d)


def kernel(x, g_mix, w_in, b_f, g_q, g_k, conv_w, conv_b, w_rg, b_rg, w_ig, b_ig, lam, g_attn_out, g_lru_out,
           w_out, g_ffn, w_gate, w_up, w_down):
    h = x
    for l in range(g_mix.shape[0]):
        h = _layer(h, g_mix[l], w_in[l], b_f[l], g_q[l], g_k[l], conv_w[l], conv_b[l], w_rg[l], b_rg[l],
                   w_ig[l], b_ig[l], lam[l], g_attn_out[l], g_lru_out[l], w_out[l], g_ffn[l], w_gate[l],
                   w_up[l], w_down[l])
    return h
```

```python
import functools
import math

import jax
import jax.numpy as jnp
from jax import lax
from jax.experimental import pallas as pl
from jax.experimental.pallas import tpu as pltpu

EPS = 1e-6
RG_C = 8.0
LOG2E = math.log2(math.e)
EXP2_ZERO = 160.0
BF16_ROUNDING_SLACK = (1.0 + 2.0 ** -8) ** 2 * 1.001
CONV_WIDTH = 4

V7X_LANES = 128
V7X_SUBLANES = 8
V7X_VMEM_LIMIT_BYTES = 56 * 1024 * 1024

F32 = jnp.float32
BF16 = jnp.bfloat16


def _tile(dim, pref):
    t = min(dim, pref)
    while dim % t:
        t //= 2
    return t


def _params(*sem):
    return pltpu.CompilerParams(dimension_semantics=sem, vmem_limit_bytes=V7X_VMEM_LIMIT_BYTES)


def _rms_scale(x):
    return lax.rsqrt(jnp.mean(x * x, axis=-1, keepdims=True) + EPS)


def _sigmoid(z):
    return 0.5 * jnp.tanh(0.5 * z) + 0.5


def _split_w_in_kernel(wt_ref, ft_ref, main_ref, f_ref, *, n_f):
    main_ref[...] = wt_ref[...].T.astype(BF16)

    @pl.when(pl.program_id(0) == 0)
    def _():
        lane = lax.broadcasted_iota(jnp.int32, f_ref.shape, 1)
        f_ref[...] = jnp.where(lane < n_f, ft_ref[...].T, 0.0).astype(BF16)


def _split_w_in(w_in_t, *, f_lo, n_f):
    n_in, d = w_in_t.shape
    n_main = n_in - n_f
    cols = _tile(math.gcd(f_lo, n_main - f_lo), 256)
    assert cols % V7X_LANES == 0 and f_lo + V7X_LANES <= n_in and n_f % V7X_SUBLANES == 0
    return pl.pallas_call(
        functools.partial(_split_w_in_kernel, n_f=n_f),
        grid=(n_main // cols,),
        in_specs=[
            pl.BlockSpec((pl.Element(cols), pl.Element(d)),
                         lambda j: (pl.multiple_of(jnp.where(j * cols < f_lo, j * cols, j * cols + n_f),
                                                   V7X_SUBLANES), 0)),
            pl.BlockSpec((pl.Element(V7X_LANES), pl.Element(d)), lambda j: (f_lo, 0)),
        ],
        out_specs=[pl.BlockSpec((d, cols), lambda j: (0, j)),
                   pl.BlockSpec((d, V7X_LANES), lambda j: (0, 0))],
        out_shape=[jax.ShapeDtypeStruct((d, n_main), BF16),
                   jax.ShapeDtypeStruct((d, V7X_LANES), BF16)],
        compiler_params=_params("arbitrary"),
        name="split_w_in",
    )(w_in_t, w_in_t)


def _inproj_kernel(x_ref, gmix_ref, w_ref, wf_ref, gain_ref, proj_ref, f_ref, xn_sc, *, n_norm_tiles, head_dim):
    j = pl.program_id(1)

    @pl.when(j == 0)
    def _():
        x = x_ref[...]
        xn = (x * _rms_scale(x) * gmix_ref[...]).astype(BF16)
        xn_sc[...] = xn
        f_ref[...] = jnp.dot(xn, wf_ref[...], preferred_element_type=F32)

    y = jnp.dot(xn_sc[...], w_ref[...], preferred_element_type=F32)

    @pl.when(j < n_norm_tiles)
    def _():
        for c in range(0, y.shape[1], head_dim):
            ys = y[:, c:c + head_dim]
            proj_ref[:, c:c + head_dim] = (ys * _rms_scale(ys) * gain_ref[:, c:c + head_dim]).astype(BF16)

    @pl.when(j >= n_norm_tiles)
    def _():
        proj_ref[...] = y.astype(BF16)


def _in_projection(x2, g_mix, w_main, w_f, gain, *, n_norm_cols, head_dim):
    m, d = x2.shape
    n = w_main.shape[1]
    tm = _tile(m, 512)
    tn = _tile(math.gcd(n, n_norm_cols), 1024)
    kern = functools.partial(_inproj_kernel, n_norm_tiles=n_norm_cols // tn, head_dim=head_dim)
    return pl.pallas_call(
        kern,
        grid=(m // tm, n // tn),
        in_specs=[
            pl.BlockSpec((tm, d), lambda i, j: (i, 0)),
            pl.BlockSpec((1, d), lambda i, j: (0, 0)),
            pl.BlockSpec((d, tn), lambda i, j: (0, j)),
            pl.BlockSpec((d, V7X_LANES), lambda i, j: (0, 0)),
            pl.BlockSpec((1, tn), lambda i, j: (0, j)),
        ],
        out_specs=[
            pl.BlockSpec((tm, tn), lambda i, j: (i, j)),
            pl.BlockSpec((tm, V7X_LANES), lambda i, j: (i, 0)),
        ],
        out_shape=[
            jax.ShapeDtypeStruct((m, n), BF16),
            jax.ShapeDtypeStruct((m, V7X_LANES), F32),
        ],
        scratch_shapes=[pltpu.VMEM((tm, d), BF16)],
        compiler_params=_params("parallel", "arbitrary"),
        name="in_projection",
    )(x2, g_mix, w_main, w_f, gain)


def _forget_cumsum_kernel(f_ref, bf_ref, c_ref):
    z = f_ref[0] + bf_ref[...]
    c = jnp.minimum(z, 0.0) - jnp.log1p(jnp.exp(-jnp.abs(z)))
    row = lax.broadcasted_iota(jnp.int32, c.shape, 0)
    d = 1
    while d < c.shape[0]:
        c = c + jnp.where(row >= d, pltpu.roll(c, d, axis=0), 0.0)
        d *= 2
    c_ref[0] = c


def _forget_cumsum(f3, b_f_pad):
    b, s, lanes = f3.shape
    return pl.pallas_call(
        _forget_cumsum_kernel,
        grid=(b,),
        in_specs=[
            pl.BlockSpec((1, s, lanes), lambda i: (i, 0, 0)),
            pl.BlockSpec((1, lanes), lambda i: (0, 0)),
        ],
        out_specs=pl.BlockSpec((1, s, lanes), lambda i: (i, 0, 0)),
        out_shape=jax.ShapeDtypeStruct((b, s, lanes), F32),
        compiler_params=_params("parallel"),
        name="forget_cumsum",
    )(f3, b_f_pad)


def _bias_columns(c, ones_first):
    hi = c.astype(BF16).astype(F32)
    mid = (c - hi).astype(BF16).astype(F32)
    lo = (c - hi - mid).astype(BF16).astype(F32)
    lane = lax.broadcasted_iota(jnp.int32, (c.shape[0], V7X_LANES), 1)
    first_c, first_one = (3, 0) if ones_first else (0, 3)
    cols = jnp.where(lane == first_c, hi, jnp.where(lane == first_c + 1, mid, jnp.where(lane == first_c + 2, lo, 0.0)))
    ones = (lane >= first_one) & (lane < first_one + 3)
    return jnp.where(ones, 1.0, cols).astype(BF16)


def _attn_kernel(*refs, tile, chunk, n_cast):
    q_ref, k_ref, v_ref, c_ref, bound_ref = refs[:5]
    w32_refs = refs[5:5 + n_cast]
    o_ref = refs[5 + n_cast]
    w16_refs = refs[6 + n_cast:6 + 2 * n_cast]
    qa_sc, ka_sc, va_sc, m_sc, acc_sc, s0_sc, s1_sc = refs[6 + 2 * n_cast:]
    for w32_ref, w16_ref in zip(w32_refs, w16_refs):
        w16_ref[...] = w32_ref[...].astype(BF16)

    h = pl.program_id(1)
    qi = pl.program_id(2)
    seq, hd = k_ref.shape
    tq = 2 * tile
    lanes = m_sc.shape[1]

    def head_c(r0, n):
        cc = c_ref[0, pl.ds(r0, n), :]
        lane = lax.broadcasted_iota(jnp.int32, cc.shape, 1)
        return LOG2E * jnp.sum(jnp.where(lane == h, cc, 0.0), axis=1, keepdims=True)

    @pl.when(qi == 0)
    def _():
        def fill(ci, carry):
            r0 = pl.multiple_of(ci * chunk, chunk)
            ka_sc[pl.ds(r0, chunk), :hd] = k_ref[pl.ds(r0, chunk), :]
            ka_sc[pl.ds(r0, chunk), hd:] = _bias_columns(-head_c(r0, chunk), ones_first=True)
            va_sc[pl.ds(r0, chunk), :hd] = v_ref[pl.ds(r0, chunk), :]
            va_sc[pl.ds(r0, chunk), hd:] = jnp.ones((chunk, va_sc.shape[1] - hd), BF16)
            return carry
        lax.fori_loop(0, seq // chunk, fill, 0)

    qa_sc[:, :hd] = q_ref[...]
    qa_sc[:, hd:] = _bias_columns(head_c(pl.multiple_of(qi * tq, tq), tq), ones_first=False)
    m_sc[...] = jnp.full(m_sc.shape, -jnp.inf, F32)
    acc_sc[...] = jnp.zeros(acc_sc.shape, F32)

    def logits(sub, kb):
        k0 = pl.multiple_of(kb * tile, tile)
        return lax.dot_general(qa_sc[sub * tile:(sub + 1) * tile, :], ka_sc[pl.ds(k0, tile), :],
                               (((1,), (1,)), ((), ())), preferred_element_type=F32)

    def update(sub, s, kb, masked):
        rows = slice(sub * tile, (sub + 1) * tile)
        k0 = pl.multiple_of(kb * tile, tile)
        if masked:
            r = lax.broadcasted_iota(jnp.int32, s.shape, 0)
            c = lax.broadcasted_iota(jnp.int32, s.shape, 1)
            s = jnp.where(c <= r, s, -jnp.inf)
        m_prev = m_sc[rows, :]
        m_new = jnp.maximum(m_prev, jnp.max(s, axis=1, keepdims=True))
        alpha = jnp.exp2(m_prev - m_new)
        p = jnp.exp2(s - jnp.tile(m_new, (1, tile // lanes)))
        pv = jnp.dot(p.astype(BF16), va_sc[pl.ds(k0, tile), :], preferred_element_type=F32)
        acc_sc[rows, :] = jnp.tile(alpha, (1, acc_sc.shape[1] // lanes)) * acc_sc[rows, :] + pv
        m_sc[rows, :] = m_new

    kd = 2 * qi
    n_blocks = seq // tile
    c_end = c_ref[0, pl.ds(tile - 1, n_blocks, stride=tile), :]
    lane = lax.broadcasted_iota(jnp.int32, c_end.shape, 1)
    c_end = LOG2E * jnp.sum(jnp.where(lane == h, c_end, 0.0), axis=1, keepdims=True)
    c_start = head_c(pl.multiple_of(qi * tq, tq), V7X_SUBLANES)[0:1, :]
    kb_idx = lax.broadcasted_iota(jnp.int32, c_end.shape, 0)
    dead = (2.0 * bound_ref[0:1, 0:1] + (c_start - c_end) <= -EXP2_ZERO) & (kb_idx < kd)
    kb0 = 2 * (jnp.sum(dead.astype(jnp.int32)) // 2)
    pairs = qi - kb0 // 2

    for sub in range(2):
        s0_sc[sub] = logits(sub, kb0)

    def block_pair(kb):
        for sub in range(2):
            s1_sc[sub] = logits(sub, kb + 1)
        for sub in range(2):
            update(sub, s0_sc[sub], kb, False)
        for sub in range(2):
            s0_sc[sub] = logits(sub, kb + 2)
        for sub in range(2):
            update(sub, s1_sc[sub], kb + 1, False)

    def two_pairs(it, carry):
        block_pair(kb0 + 4 * it)
        block_pair(kb0 + 4 * it + 2)
        return carry

    def one_pair(it, carry):
        block_pair(kb0 + 4 * (pairs // 2))
        return carry

    lax.fori_loop(0, pairs // 2, two_pairs, 0)
    lax.fori_loop(0, pairs % 2, one_pair, 0)
    s_last = logits(1, kd + 1)
    update(0, s0_sc[0], kd, True)
    update(1, s0_sc[1], kd, False)
    update(1, s_last, kd + 1, True)
    o_ref[...] = acc_sc[:, :hd] / acc_sc[:, hd:2 * hd]


def _cast_block_spec(rows, cols, n_steps, step_of):
    bf16_rows = 2 * V7X_SUBLANES
    br = next(r for r in range(bf16_rows, rows + 1, bf16_rows) if rows % r == 0 and rows // r <= n_steps)
    last = rows // br - 1
    return pl.BlockSpec((br, cols), lambda b, h, i: (jnp.minimum(step_of(b, h, i), last), 0))


def _attention(proj, c_col, qk_bound, weights_f32, *, batch, seq, n_heads, head_dim):
    assert head_dim == V7X_LANES
    tile = _tile(seq, 512)
    tq = 2 * tile
    assert seq % tq == 0
    nq = seq // tq
    attn_w = n_heads * head_dim
    n_steps = batch * n_heads * nq
    step_of = lambda b, h, i: (b * n_heads + h) * nq + i
    cast_specs = [_cast_block_spec(*w.shape, n_steps, step_of) for w in weights_f32]
    kern = functools.partial(_attn_kernel, tile=tile, chunk=tile, n_cast=len(weights_f32))
    return pl.pallas_call(
        kern,
        grid=(batch, n_heads, nq),
        in_specs=[
            pl.BlockSpec((tq, head_dim), lambda b, h, i: (b * nq + i, h)),
            pl.BlockSpec((seq, head_dim), lambda b, h, i: (b, n_heads + h)),
            pl.BlockSpec((seq, head_dim), lambda b, h, i: (b, 2 * n_heads + h)),
            pl.BlockSpec((1, seq, V7X_LANES), lambda b, h, i: (b, 0, 0)),
            pl.BlockSpec((1, V7X_LANES), lambda b, h, i: (0, 0)),
        ] + cast_specs,
        out_specs=[pl.BlockSpec((tq, head_dim), lambda b, h, i: (b * nq + i, h))] + cast_specs,
        out_shape=[jax.ShapeDtypeStruct((batch * seq, attn_w), F32)]
        + [jax.ShapeDtypeStruct(w.shape, BF16) for w in weights_f32],
        scratch_shapes=[
            pltpu.VMEM((tq, 2 * head_dim), BF16),
            pltpu.VMEM((seq, 2 * head_dim), BF16),
            pltpu.VMEM((seq, 2 * head_dim), BF16),
            pltpu.VMEM((tq, V7X_LANES), F32),
            pltpu.VMEM((tq, 2 * head_dim), F32),
            pltpu.VMEM((2, tile, tile), F32),
            pltpu.VMEM((2, tile, tile), F32),
        ],
        compiler_params=_params("parallel", "parallel", "arbitrary"),
        name="fox_attention",
    )(proj, proj, proj, c_col, qk_bound, *weights_f32)


def _lru_kernel(ux_ref, uy_ref, cw_ref, cb_ref, wg_ref, brg_ref, big_ref, lam_ref, gout_ref, o_ref,
                tail_sc, hlast_sc, xr_sc, a_sc, b_sc, *, block):
    t = pl.program_id(1)
    rows, width = xr_sc.shape
    sub = V7X_SUBLANES

    @pl.when(t == 0)
    def _():
        tail_sc[...] = jnp.zeros(tail_sc.shape, F32)
        hlast_sc[...] = jnp.zeros(hlast_sc.shape, F32)

    u = ux_ref[...].astype(F32)
    cw = cw_ref[...]
    cb = cb_ref[...]
    xr = cw[CONV_WIDTH - 1:CONV_WIDTH, :] * u + cb
    for d in range(1, CONV_WIDTH):
        xr = xr + cw[CONV_WIDTH - 1 - d:CONV_WIDTH - d, :] * pltpu.roll(u, d, axis=0)
    xr_sc[...] = xr
    u_head = u[0:sub, :]
    tail = tail_sc[...]
    row8 = lax.broadcasted_iota(jnp.int32, (sub, width), 0)
    xr_head = cw[CONV_WIDTH - 1:CONV_WIDTH, :] * u_head + cb
    for d in range(1, CONV_WIDTH):
        shifted = jnp.where(row8 < d, pltpu.roll(tail, d, axis=0), pltpu.roll(u_head, d, axis=0))
        xr_head = xr_head + cw[CONV_WIDTH - 1 - d:CONV_WIDTH - d, :] * shifted
    xr_sc[0:sub, :] = xr_head
    tail_sc[...] = u[rows - sub:rows, :]

    lam = lam_ref[...]
    log2_a_max = (-RG_C * LOG2E) * (jnp.maximum(-lam, 0.0) + jnp.log1p(jnp.exp(-jnp.abs(lam))))

    sub_idx = lax.broadcasted_iota(jnp.int32, (rows // sub, sub, block), 1)
    for n in range(width // block):
        cs = slice(n * block, (n + 1) * block)
        xb = xr_sc[:, cs]
        g = jnp.dot(xb.astype(BF16), wg_ref[n], preferred_element_type=F32)
        r = _sigmoid(g[:, :block] + brg_ref[:, cs])
        i = _sigmoid(g[:, block:] + big_ref[:, cs])
        a = jnp.exp2(r * log2_a_max[:, cs])
        b = jnp.sqrt(1.0 - a * a) * (i * xb)
        a3 = a.reshape(rows // sub, sub, block)
        b3 = b.reshape(rows // sub, sub, block)
        d = 1
        while d < sub:
            ok = sub_idx >= d
            b3 = jnp.where(ok, a3 * pltpu.roll(b3, d, axis=1) + b3, b3)
            a3 = jnp.where(ok, a3 * pltpu.roll(a3, d, axis=1), a3)
            d *= 2
        a_sc[:, cs] = a3.reshape(rows, block)
        b_sc[:, cs] = b3.reshape(rows, block)

    def group(gi, h_prev):
        r0 = pl.multiple_of(gi * sub, sub)
        h8 = a_sc[pl.ds(r0, sub), :] * h_prev + b_sc[pl.ds(r0, sub), :]
        b_sc[pl.ds(r0, sub), :] = h8
        return jnp.broadcast_to(h8[sub - 1:sub, :], h8.shape)

    hlast_sc[...] = lax.fori_loop(0, rows // sub, group, hlast_sc[...])

    uy = uy_ref[...].astype(F32)
    gelu = 0.5 * uy * (1.0 + jnp.tanh(math.sqrt(2.0 / math.pi) * (uy + 0.044715 * (uy * uy * uy))))
    y = gelu * b_sc[...]
    o_ref[...] = (y * _rms_scale(y) * gout_ref[...]).astype(BF16)


def _rglru(proj, conv_w, conv_b, w_gates, b_rg, b_ig, lam, g_out, *, batch, seq, width, ux_col):
    nb, block, _ = w_gates.shape
    rows = _tile(seq, 256)
    nt = seq // rows
    ux_blk = ux_col // width
    vec = lambda: pl.BlockSpec((1, width), lambda b, t: (0, 0))
    kern = functools.partial(_lru_kernel, block=block)
    return pl.pallas_call(
        kern,
        grid=(batch, nt),
        in_specs=[
            pl.BlockSpec((rows, width), lambda b, t: (b * nt + t, ux_blk)),
            pl.BlockSpec((rows, width), lambda b, t: (b * nt + t, ux_blk + 1)),
            pl.BlockSpec((CONV_WIDTH, width), lambda b, t: (0, 0)),
            vec(),
            pl.BlockSpec((nb, block, 2 * block), lambda b, t: (0, 0, 0)),
            vec(), vec(), vec(), vec(),
        ],
        out_specs=pl.BlockSpec((rows, width), lambda b, t: (b * nt + t, 0)),
        out_shape=jax.ShapeDtypeStruct((batch * seq, width), BF16),
        scratch_shapes=[
            pltpu.VMEM((V7X_SUBLANES, width), F32),
            pltpu.VMEM((V7X_SUBLANES, width), F32),
            pltpu.VMEM((rows, width), F32),
            pltpu.VMEM((rows, width), F32),
            pltpu.VMEM((rows, width), F32),
        ],
        compiler_params=_params("parallel", "arbitrary"),
        name="rglru",
    )(proj, proj, conv_w, conv_b, w_gates, b_rg, b_ig, lam, g_out)


def _outproj_kernel(attn_ref, lru_ref, gattn_ref, w_ref, x_ref, h_ref, mixed_sc):
    @pl.when(pl.program_id(1) == 0)
    def _():
        a = attn_ref[...]
        wa = a.shape[1]
        mixed_sc[:, :wa] = (a * _rms_scale(a) * gattn_ref[...]).astype(BF16)
        mixed_sc[:, wa:] = lru_ref[...]

    h_ref[...] = x_ref[...] + jnp.dot(mixed_sc[...], w_ref[...], preferred_element_type=F32)


def _out_projection(attn, lru, g_attn, w_out, x2):
    m, wa = attn.shape
    wl = lru.shape[1]
    d = w_out.shape[1]
    tm = _tile(m, 512)
    tn = _tile(d, 1024)
    return pl.pallas_call(
        _outproj_kernel,
        grid=(m // tm, d // tn),
        in_specs=[
            pl.BlockSpec((tm, wa), lambda i, j: (i, 0)),
            pl.BlockSpec((tm, wl), lambda i, j: (i, 0)),
            pl.BlockSpec((1, wa), lambda i, j: (0, 0)),
            pl.BlockSpec((wa + wl, tn), lambda i, j: (0, j)),
            pl.BlockSpec((tm, tn), lambda i, j: (i, j)),
        ],
        out_specs=pl.BlockSpec((tm, tn), lambda i, j: (i, j)),
        out_shape=jax.ShapeDtypeStruct((m, d), F32),
        scratch_shapes=[pltpu.VMEM((tm, wa + wl), BF16)],
        compiler_params=_params("parallel", "arbitrary"),
        name="out_projection",
    )(attn, lru, g_attn, w_out, x2)


def _ffn_kernel(h_ref, gffn_ref, wg_ref, wu_ref, wd_ref, o_ref, hn_sc):
    @pl.when(pl.program_id(1) == 0)
    def _():
        rows = _tile(h_ref.shape[0], 128)
        for r in range(0, h_ref.shape[0], rows):
            h = h_ref[r:r + rows, :]
            hn_sc[r:r + rows, :] = (h * _rms_scale(h) * gffn_ref[...]).astype(BF16)
            o_ref[r:r + rows, :] = h

    hn = hn_sc[...]
    g = jnp.dot(hn, wg_ref[...], preferred_element_type=F32)
    u = jnp.dot(hn, wu_ref[...], preferred_element_type=F32)
    act = (g * (1.0 / (1.0 + jnp.exp(-g))) * u).astype(BF16)
    chunk = _tile(o_ref.shape[1], 1024)
    for c in range(0, o_ref.shape[1], chunk):
        o_ref[:, c:c + chunk] += jnp.dot(act, wd_ref[:, c:c + chunk], preferred_element_type=F32)


def _ffn(h, g_ffn, w_gate, w_up, w_down):
    m, d = h.shape
    dff = w_gate.shape[1]
    tm = _tile(m, 512)
    tf = _tile(dff, 256)
    return pl.pallas_call(
        _ffn_kernel,
        grid=(m // tm, dff // tf),
        in_specs=[
            pl.BlockSpec((tm, d), lambda i, f: (i, 0)),
            pl.BlockSpec((1, d), lambda i, f: (0, 0)),
            pl.BlockSpec((d, tf), lambda i, f: (0, f)),
            pl.BlockSpec((d, tf), lambda i, f: (0, f)),
            pl.BlockSpec((tf, d), lambda i, f: (f, 0)),
        ],
        out_specs=pl.BlockSpec((tm, d), lambda i, f: (i, 0)),
        out_shape=jax.ShapeDtypeStruct((m, d), F32),
        scratch_shapes=[pltpu.VMEM((tm, d), BF16)],
        compiler_params=_params("parallel", "arbitrary"),
        name="swiglu_ffn",
    )(h, g_ffn, w_gate, w_up, w_down)


def _layer(x, g_mix, w_in, b_f, g_q, g_k, conv_w, conv_b, w_rg, b_rg, w_ig, b_ig, lam,
           g_attn_out, g_lru_out, w_out, g_ffn, w_gate, w_up, w_down):
    batch, seq, d = x.shape
    n_heads = b_f.shape[0]
    head_dim = g_q.shape[0]
    attn_w = n_heads * head_dim
    lru_w = lam.shape[0]
    row = lambda v: v.reshape(1, -1).astype(F32)

    f_lo, f_hi = 3 * attn_w, 3 * attn_w + n_heads
    w_main, w_f = _split_w_in(jnp.swapaxes(w_in, 0, 1), f_lo=f_lo, n_f=n_heads)
    b_f_pad = jnp.pad(b_f.astype(F32), (0, V7X_LANES - n_heads)).reshape(1, V7X_LANES)
    qk_gain = jnp.concatenate([jnp.tile(g_q.astype(F32) * (head_dim ** -0.5 * LOG2E), n_heads),
                               jnp.tile(g_k.astype(F32), n_heads),
                               jnp.ones((w_main.shape[1] - 2 * attn_w,), F32)]).reshape(1, -1)
    w_gates = jnp.concatenate([w_rg, w_ig], axis=-1).astype(BF16)

    x2 = x.reshape(batch * seq, d)
    proj, f_logit = _in_projection(x2, row(g_mix), w_main, w_f, qk_gain,
                                   n_norm_cols=2 * attn_w, head_dim=head_dim)
    c_col = _forget_cumsum(f_logit.reshape(batch, seq, V7X_LANES), b_f_pad)
    qk_bound = (head_dim * BF16_ROUNDING_SLACK) * jnp.max(jnp.abs(qk_gain[:, :attn_w])) * jnp.max(jnp.abs(g_k.astype(F32)))
    qk_bound = jnp.full((1, V7X_LANES), qk_bound, F32)
    attn, w_out16, w_gate16, w_up16, w_down16 = _attention(
        proj, c_col, qk_bound, [w.astype(F32) for w in (w_out, w_gate, w_up, w_down)],
        batch=batch, seq=seq, n_heads=n_heads, head_dim=head_dim)
    lru = _rglru(proj, conv_w.astype(F32), row(conv_b), w_gates, row(b_rg), row(b_ig), row(lam),
                 row(g_lru_out), batch=batch, seq=seq, width=lru_w, ux_col=3 * attn_w)
    h = _out_projection(attn, lru, row(g_attn_out), w_out16, x2)
    out = _ffn(h, row(g_ffn), w_gate16, w_up16, w_down16)
    return out.reshape(batch, seq, d)


def kernel(x, g_mix, w_in, b_f, g_q, g_k, conv_w, conv_b, w_rg, b_rg, w_ig, b_ig, lam, g_attn_out, g_lru_out,
           w_out, g_ffn, w_gate, w_up, w_down):
    h = x
    for l in range(g_mix.shape[0]):
        h = _layer(h, g_mix[l], w_in[l], b_f[l], g_q[l], g_k[l], conv_w[l], conv_b[l], w_rg[l], b_rg[l],
                   w_ig[l], b_ig[l], lam[l], g_attn_out[l], g_lru_out[l], w_out[l], g_ffn[l], w_gate[l],
                   w_up[l], w_down[l])
    return h
```

```python
import functools
import math

import jax
import jax.numpy as jnp
from jax import lax
from jax.experimental import pallas as pl
from jax.experimental.pallas import tpu as pltpu

EPS = 1e-6
RG_C = 8.0
LOG2E = math.log2(math.e)
EXP2_ZERO = 160.0
BF16_ROUNDING_SLACK = (1.0 + 2.0 ** -8) ** 2 * 1.001
CONV_WIDTH = 4

V7X_LANES = 128
V7X_SUBLANES = 8
V7X_VMEM_LIMIT_BYTES = 56 * 1024 * 1024

F32 = jnp.float32
BF16 = jnp.bfloat16


def _tile(dim, pref):
    t = min(dim, pref)
    while dim % t:
        t //= 2
    return t


def _params(*sem):
    return pltpu.CompilerParams(dimension_semantics=sem, vmem_limit_bytes=V7X_VMEM_LIMIT_BYTES)


def _rms_scale(x):
    return lax.rsqrt(jnp.mean(x * x, axis=-1, keepdims=True) + EPS)


def _sigmoid(z):
    return 0.5 * jnp.tanh(0.5 * z) + 0.5


def _split_w_in_kernel(wt_ref, ft_ref, main_ref, f_ref, *, n_f):
    main_ref[...] = wt_ref[...].T.astype(BF16)

    @pl.when(pl.program_id(0) == 0)
    def _():
        lane = lax.broadcasted_iota(jnp.int32, f_ref.shape, 1)
        f_ref[...] = jnp.where(lane < n_f, ft_ref[...].T, 0.0).astype(BF16)


def _split_w_in(w_in_t, *, f_lo, n_f):
    n_in, d = w_in_t.shape
    n_main = n_in - n_f
    cols = _tile(math.gcd(f_lo, n_main - f_lo), 256)
    assert cols % V7X_LANES == 0 and f_lo + V7X_LANES <= n_in and n_f % V7X_SUBLANES == 0
    return pl.pallas_call(
        functools.partial(_split_w_in_kernel, n_f=n_f),
        grid=(n_main // cols,),
        in_specs=[
            pl.BlockSpec((pl.Element(cols), pl.Element(d)),
                         lambda j: (pl.multiple_of(jnp.where(j * cols < f_lo, j * cols, j * cols + n_f),
                                                   V7X_SUBLANES), 0)),
            pl.BlockSpec((pl.Element(V7X_LANES), pl.Element(d)), lambda j: (f_lo, 0)),
        ],
        out_specs=[pl.BlockSpec((d, cols), lambda j: (0, j)),
                   pl.BlockSpec((d, V7X_LANES), lambda j: (0, 0))],
        out_shape=[jax.ShapeDtypeStruct((d, n_main), BF16),
                   jax.ShapeDtypeStruct((d, V7X_LANES), BF16)],
        compiler_params=_params("arbitrary"),
        name="split_w_in",
    )(w_in_t, w_in_t)


def _inproj_kernel(x_ref, gmix_ref, w_ref, wf_ref, gain_ref, proj_ref, f_ref, xn_sc, *, n_norm_tiles, head_dim):
    j = pl.program_id(1)

    @pl.when(j == 0)
    def _():
        x = x_ref[...]
        xn = (x * _rms_scale(x) * gmix_ref[...]).astype(BF16)
        xn_sc[...] = xn
        f_ref[...] = jnp.dot(xn, wf_ref[...], preferred_element_type=F32)

    @pl.when(j < n_norm_tiles)
    def _():
        step = 2 * head_dim
        for c0 in range(0, proj_ref.shape[1], step):
            y = jnp.dot(xn_sc[...], w_ref[:, c0:c0 + step], preferred_element_type=F32)
            for c in range(c0, c0 + step, head_dim):
                ys = y[:, c - c0:c - c0 + head_dim]
                proj_ref[:, c:c + head_dim] = (ys * _rms_scale(ys) * gain_ref[:, c:c + head_dim]).astype(BF16)

    @pl.when(j >= n_norm_tiles)
    def _():
        proj_ref[...] = jnp.dot(xn_sc[...], w_ref[...], preferred_element_type=F32).astype(BF16)


def _in_projection(x2, g_mix, w_main, w_f, gain, *, n_norm_cols, head_dim):
    m, d = x2.shape
    n = w_main.shape[1]
    tm = _tile(m, 512)
    tn = _tile(math.gcd(n, n_norm_cols), 1024)
    kern = functools.partial(_inproj_kernel, n_norm_tiles=n_norm_cols // tn, head_dim=head_dim)
    return pl.pallas_call(
        kern,
        grid=(m // tm, n // tn),
        in_specs=[
            pl.BlockSpec((tm, d), lambda i, j: (i, 0)),
            pl.BlockSpec((1, d), lambda i, j: (0, 0)),
            pl.BlockSpec((d, tn), lambda i, j: (0, j)),
            pl.BlockSpec((d, V7X_LANES), lambda i, j: (0, 0)),
            pl.BlockSpec((1, tn), lambda i, j: (0, j)),
        ],
        out_specs=[
            pl.BlockSpec((tm, tn), lambda i, j: (i, j)),
            pl.BlockSpec((tm, V7X_LANES), lambda i, j: (i, 0)),
        ],
        out_shape=[
            jax.ShapeDtypeStruct((m, n), BF16),
            jax.ShapeDtypeStruct((m, V7X_LANES), F32),
        ],
        scratch_shapes=[pltpu.VMEM((tm, d), BF16)],
        compiler_params=_params("parallel", "arbitrary"),
        name="in_projection",
    )(x2, g_mix, w_main, w_f, gain)


def _forget_cumsum_kernel(f_ref, bf_ref, c_ref):
    z = f_ref[0] + bf_ref[...]
    c = jnp.minimum(z, 0.0) - jnp.log1p(jnp.exp(-jnp.abs(z)))
    row = lax.broadcasted_iota(jnp.int32, c.shape, 0)
    d = 1
    while d < c.shape[0]:
        c = c + jnp.where(row >= d, pltpu.roll(c, d, axis=0), 0.0)
        d *= 2
    c_ref[0] = c


def _forget_cumsum(f3, b_f_pad):
    b, s, lanes = f3.shape
    return pl.pallas_call(
        _forget_cumsum_kernel,
        grid=(b,),
        in_specs=[
            pl.BlockSpec((1, s, lanes), lambda i: (i, 0, 0)),
            pl.BlockSpec((1, lanes), lambda i: (0, 0)),
        ],
        out_specs=pl.BlockSpec((1, s, lanes), lambda i: (i, 0, 0)),
        out_shape=jax.ShapeDtypeStruct((b, s, lanes), F32),
        compiler_params=_params("parallel"),
        name="forget_cumsum",
    )(f3, b_f_pad)


def _bias_columns(c, ones_first):
    hi = c.astype(BF16).astype(F32)
    mid = (c - hi).astype(BF16).astype(F32)
    lo = (c - hi - mid).astype(BF16).astype(F32)
    lane = lax.broadcasted_iota(jnp.int32, (c.shape[0], V7X_LANES), 1)
    first_c, first_one = (3, 0) if ones_first else (0, 3)
    cols = jnp.where(lane == first_c, hi, jnp.where(lane == first_c + 1, mid, jnp.where(lane == first_c + 2, lo, 0.0)))
    ones = (lane >= first_one) & (lane < first_one + 3)
    return jnp.where(ones, 1.0, cols).astype(BF16)


def _attn_kernel(*refs, tile, chunk, n_cast):
    q_ref, k_ref, v_ref, c_ref, bound_ref = refs[:5]
    w32_refs = refs[5:5 + n_cast]
    o_ref = refs[5 + n_cast]
    w16_refs = refs[6 + n_cast:6 + 2 * n_cast]
    qa_sc, ka_sc, va_sc, m_sc, acc_sc, s0_sc, s1_sc, s2_sc = refs[6 + 2 * n_cast:]

    h = pl.program_id(1)
    qi = pl.program_id(2)
    seq, hd = k_ref.shape
    tq = 2 * tile
    lanes = m_sc.shape[1]

    def head_c(r0, n):
        cc = c_ref[0, pl.ds(r0, n), :]
        lane = lax.broadcasted_iota(jnp.int32, cc.shape, 1)
        return LOG2E * jnp.sum(jnp.where(lane == h, cc, 0.0), axis=1, keepdims=True)

    @pl.when(qi == 0)
    def _():
        def fill(ci, carry):
            r0 = pl.multiple_of(ci * chunk, chunk)
            ka_sc[pl.ds(r0, chunk), :hd] = k_ref[pl.ds(r0, chunk), :]
            ka_sc[pl.ds(r0, chunk), hd:] = _bias_columns(-head_c(r0, chunk), ones_first=True)
            va_sc[pl.ds(r0, chunk), :hd] = v_ref[pl.ds(r0, chunk), :]
            va_sc[pl.ds(r0, chunk), hd:] = jnp.ones((chunk, va_sc.shape[1] - hd), BF16)
            return carry
        lax.fori_loop(0, seq // chunk, fill, 0)

    qa_sc[:, :hd] = q_ref[...]
    qa_sc[:, hd:] = _bias_columns(head_c(pl.multiple_of(qi * tq, tq), tq), ones_first=False)
    m_sc[...] = jnp.full(m_sc.shape, -jnp.inf, F32)
    acc_sc[...] = jnp.zeros(acc_sc.shape, F32)

    def logits(sub, kb):
        k0 = pl.multiple_of(kb * tile, tile)
        return lax.dot_general(qa_sc[sub * tile:(sub + 1) * tile, :], ka_sc[pl.ds(k0, tile), :],
                               (((1,), (1,)), ((), ())), preferred_element_type=F32)

    def update(sub, *blocks):
        rows = slice(sub * tile, (sub + 1) * tile)
        m_prev = m_sc[rows, :]
        m_new = m_prev
        logit_blocks = []
        for s, kb, masked in blocks:
            if masked:
                r = lax.broadcasted_iota(jnp.int32, s.shape, 0)
                c = lax.broadcasted_iota(jnp.int32, s.shape, 1)
                s = jnp.where(c <= r, s, -jnp.inf)
            logit_blocks.append(s)
            m_new = jnp.maximum(m_new, jnp.max(s, axis=1, keepdims=True))
        m_wide = jnp.tile(m_new, (1, tile // lanes))
        pv = None
        for s, (_, kb, _) in zip(logit_blocks, blocks):
            k0 = pl.multiple_of(kb * tile, tile)
            term = jnp.dot(jnp.exp2(s - m_wide).astype(BF16), va_sc[pl.ds(k0, tile), :],
                           preferred_element_type=F32)
            pv = term if pv is None else pv + term
        alpha = jnp.exp2(m_prev - m_new)
        acc_sc[rows, :] = jnp.tile(alpha, (1, acc_sc.shape[1] // lanes)) * acc_sc[rows, :] + pv
        m_sc[rows, :] = m_new

    kd = 2 * qi
    n_blocks = seq // tile
    c_end = c_ref[0, pl.ds(tile - 1, n_blocks, stride=tile), :]
    lane = lax.broadcasted_iota(jnp.int32, c_end.shape, 1)
    c_end = LOG2E * jnp.sum(jnp.where(lane == h, c_end, 0.0), axis=1, keepdims=True)
    c_start = head_c(pl.multiple_of(qi * tq, tq), V7X_SUBLANES)[0:1, :]
    kb_idx = lax.broadcasted_iota(jnp.int32, c_end.shape, 0)
    dead = (2.0 * bound_ref[0:1, 0:1] + (c_start - c_end) <= -EXP2_ZERO) & (kb_idx < kd)
    kb0 = 2 * (jnp.sum(dead.astype(jnp.int32)) // 2)
    pairs = qi - kb0 // 2

    for sub in range(2):
        s0_sc[sub] = logits(sub, kb0)
    s2_sc[...] = logits(1, kd + 1)

    def block_pair(kb):
        for sub in range(2):
            s1_sc[sub] = logits(sub, kb + 1)
        for sub in range(2):
            update(sub, (s0_sc[sub], kb, False))
        for sub in range(2):
            s0_sc[sub] = logits(sub, kb + 2)
        for sub in range(2):
            update(sub, (s1_sc[sub], kb + 1, False))

    def two_pairs(it, carry):
        block_pair(kb0 + 4 * it)
        block_pair(kb0 + 4 * it + 2)
        return carry

    def one_pair(it, carry):
        block_pair(kb0 + 4 * (pairs // 2))
        return carry

    lax.fori_loop(0, pairs // 2, two_pairs, 0)
    lax.fori_loop(0, pairs % 2, one_pair, 0)
    update(0, (s0_sc[0], kd, True))
    update(1, (s0_sc[1], kd, False), (s2_sc[...], kd + 1, True))
    o_ref[...] = acc_sc[:, :hd] / acc_sc[:, hd:2 * hd]
    for w32_ref, w16_ref in zip(w32_refs, w16_refs):
        w16_ref[...] = w32_ref[...].astype(BF16)


def _cast_block_spec(rows, cols, n_steps, step_of):
    bf16_rows = 2 * V7X_SUBLANES
    br = next(r for r in range(bf16_rows, rows + 1, bf16_rows) if rows % r == 0 and rows // r <= n_steps)
    last = rows // br - 1
    return pl.BlockSpec((br, cols), lambda *idx: (jnp.minimum(step_of(*idx), last), 0))


def _attention(proj, c_col, qk_bound, weights_f32, *, batch, seq, n_heads, head_dim):
    assert head_dim == V7X_LANES
    tile = _tile(seq, 512)
    tq = 2 * tile
    assert seq % tq == 0
    nq = seq // tq
    attn_w = n_heads * head_dim
    n_steps = batch * n_heads * nq
    step_of = lambda b, h, i: (b * n_heads + h) * nq + i
    cast_specs = [_cast_block_spec(*w.shape, n_steps, step_of) for w in weights_f32]
    kern = functools.partial(_attn_kernel, tile=tile, chunk=tile, n_cast=len(weights_f32))
    return pl.pallas_call(
        kern,
        grid=(batch, n_heads, nq),
        in_specs=[
            pl.BlockSpec((tq, head_dim), lambda b, h, i: (b * nq + i, h)),
            pl.BlockSpec((seq, head_dim), lambda b, h, i: (b, n_heads + h)),
            pl.BlockSpec((seq, head_dim), lambda b, h, i: (b, 2 * n_heads + h)),
            pl.BlockSpec((1, seq, V7X_LANES), lambda b, h, i: (b, 0, 0)),
            pl.BlockSpec((1, V7X_LANES), lambda b, h, i: (0, 0)),
        ] + cast_specs,
        out_specs=[pl.BlockSpec((tq, head_dim), lambda b, h, i: (b * nq + i, h))] + cast_specs,
        out_shape=[jax.ShapeDtypeStruct((batch * seq, attn_w), F32)]
        + [jax.ShapeDtypeStruct(w.shape, BF16) for w in weights_f32],
        scratch_shapes=[
            pltpu.VMEM((tq, 2 * head_dim), BF16),
            pltpu.VMEM((seq, 2 * head_dim), BF16),
            pltpu.VMEM((seq, 2 * head_dim), BF16),
            pltpu.VMEM((tq, V7X_LANES), F32),
            pltpu.VMEM((tq, 2 * head_dim), F32),
            pltpu.VMEM((2, tile, tile), F32),
            pltpu.VMEM((2, tile, tile), F32),
            pltpu.VMEM((tile, tile), F32),
        ],
        compiler_params=_params("parallel", "parallel", "arbitrary"),
        name="fox_attention",
    )(proj, proj, proj, c_col, qk_bound, *weights_f32)


def _lru_kernel(*refs, block, n_cast):
    ux_ref, uy_ref, cw_ref, cb_ref, wg_ref, brg_ref, big_ref, lam_ref, gout_ref = refs[:9]
    w32_refs = refs[9:9 + n_cast]
    o_ref = refs[9 + n_cast]
    w16_refs = refs[10 + n_cast:10 + 2 * n_cast]
    tail_sc, hlast_sc, xr_sc, a_sc, b_sc = refs[10 + 2 * n_cast:]
    for w32_ref, w16_ref in zip(w32_refs, w16_refs):
        w16_ref[...] = w32_ref[...].astype(BF16)

    t = pl.program_id(1)
    rows, width = xr_sc.shape
    sub = V7X_SUBLANES

    @pl.when(t == 0)
    def _():
        tail_sc[...] = jnp.zeros(tail_sc.shape, F32)
        hlast_sc[...] = jnp.zeros(hlast_sc.shape, F32)

    u = ux_ref[...].astype(F32)
    cw = cw_ref[...]
    cb = cb_ref[...]
    xr = cw[CONV_WIDTH - 1:CONV_WIDTH, :] * u + cb
    for d in range(1, CONV_WIDTH):
        xr = xr + cw[CONV_WIDTH - 1 - d:CONV_WIDTH - d, :] * pltpu.roll(u, d, axis=0)
    xr_sc[...] = xr
    u_head = u[0:sub, :]
    tail = tail_sc[...]
    row8 = lax.broadcasted_iota(jnp.int32, (sub, width), 0)
    xr_head = cw[CONV_WIDTH - 1:CONV_WIDTH, :] * u_head + cb
    for d in range(1, CONV_WIDTH):
        shifted = jnp.where(row8 < d, pltpu.roll(tail, d, axis=0), pltpu.roll(u_head, d, axis=0))
        xr_head = xr_head + cw[CONV_WIDTH - 1 - d:CONV_WIDTH - d, :] * shifted
    xr_sc[0:sub, :] = xr_head
    tail_sc[...] = u[rows - sub:rows, :]

    lam = lam_ref[...]
    log2_a_max = (-RG_C * LOG2E) * (jnp.maximum(-lam, 0.0) + jnp.log1p(jnp.exp(-jnp.abs(lam))))

    sub_idx = lax.broadcasted_iota(jnp.int32, (rows // sub, sub, block), 1)
    for n in range(width // block):
        cs = slice(n * block, (n + 1) * block)
        xb = xr_sc[:, cs]
        g = jnp.dot(xb.astype(BF16), wg_ref[n], preferred_element_type=F32)
        r = _sigmoid(g[:, :block] + brg_ref[:, cs])
        i = _sigmoid(g[:, block:] + big_ref[:, cs])
        a = jnp.exp2(r * log2_a_max[:, cs])
        b = jnp.sqrt(1.0 - a * a) * (i * xb)
        a3 = a.reshape(rows // sub, sub, block)
        b3 = b.reshape(rows // sub, sub, block)
        d = 1
        while d < sub:
            ok = sub_idx >= d
            b3 = jnp.where(ok, a3 * pltpu.roll(b3, d, axis=1) + b3, b3)
            a3 = jnp.where(ok, a3 * pltpu.roll(a3, d, axis=1), a3)
            d *= 2
        a_sc[:, cs] = a3.reshape(rows, block)
        b_sc[:, cs] = b3.reshape(rows, block)

    def group(gi, h_prev):
        r0 = pl.multiple_of(gi * sub, sub)
        h8 = a_sc[pl.ds(r0, sub), :] * h_prev + b_sc[pl.ds(r0, sub), :]
        b_sc[pl.ds(r0, sub), :] = h8
        return jnp.broadcast_to(h8[sub - 1:sub, :], h8.shape)

    hlast_sc[...] = lax.fori_loop(0, rows // sub, group, hlast_sc[...])

    uy = uy_ref[...].astype(F32)
    gelu = 0.5 * uy * (1.0 + jnp.tanh(math.sqrt(2.0 / math.pi) * (uy + 0.044715 * (uy * uy * uy))))
    y = gelu * b_sc[...]
    o_ref[...] = (y * _rms_scale(y) * gout_ref[...]).astype(BF16)


def _rglru(proj, conv_w, conv_b, w_gates, b_rg, b_ig, lam, g_out, weights_f32, *, batch, seq, width, ux_col):
    nb, block, _ = w_gates.shape
    rows = _tile(seq, 256)
    nt = seq // rows
    ux_blk = ux_col // width
    vec = lambda: pl.BlockSpec((1, width), lambda b, t: (0, 0))
    cast_specs = [_cast_block_spec(*w.shape, batch * nt, lambda b, t: b * nt + t) for w in weights_f32]
    kern = functools.partial(_lru_kernel, block=block, n_cast=len(weights_f32))
    return pl.pallas_call(
        kern,
        grid=(batch, nt),
        in_specs=[
            pl.BlockSpec((rows, width), lambda b, t: (b * nt + t, ux_blk)),
            pl.BlockSpec((rows, width), lambda b, t: (b * nt + t, ux_blk + 1)),
            pl.BlockSpec((CONV_WIDTH, width), lambda b, t: (0, 0)),
            vec(),
            pl.BlockSpec((nb, block, 2 * block), lambda b, t: (0, 0, 0)),
            vec(), vec(), vec(), vec(),
        ] + cast_specs,
        out_specs=[pl.BlockSpec((rows, width), lambda b, t: (b * nt + t, 0))] + cast_specs,
        out_shape=[jax.ShapeDtypeStruct((batch * seq, width), BF16)]
        + [jax.ShapeDtypeStruct(w.shape, BF16) for w in weights_f32],
        scratch_shapes=[
            pltpu.VMEM((V7X_SUBLANES, width), F32),
            pltpu.VMEM((V7X_SUBLANES, width), F32),
            pltpu.VMEM((rows, width), F32),
            pltpu.VMEM((rows, width), F32),
            pltpu.VMEM((rows, width), F32),
        ],
        compiler_params=_params("parallel", "arbitrary"),
        name="rglru",
    )(proj, proj, conv_w, conv_b, w_gates, b_rg, b_ig, lam, g_out, *weights_f32)


def _outproj_kernel(attn_ref, lru_ref, gattn_ref, w_ref, x_ref, h_ref, mixed_sc):
    @pl.when(pl.program_id(1) == 0)
    def _():
        a = attn_ref[...]
        wa = a.shape[1]
        mixed_sc[:, :wa] = (a * _rms_scale(a) * gattn_ref[...]).astype(BF16)
        mixed_sc[:, wa:] = lru_ref[...]

    h_ref[...] = x_ref[...] + jnp.dot(mixed_sc[...], w_ref[...], preferred_element_type=F32)


def _out_projection(attn, lru, g_attn, w_out, x2):
    m, wa = attn.shape
    wl = lru.shape[1]
    d = w_out.shape[1]
    tm = _tile(m, 512)
    tn = _tile(d, 1024)
    return pl.pallas_call(
        _outproj_kernel,
        grid=(m // tm, d // tn),
        in_specs=[
            pl.BlockSpec((tm, wa), lambda i, j: (i, 0)),
            pl.BlockSpec((tm, wl), lambda i, j: (i, 0)),
            pl.BlockSpec((1, wa), lambda i, j: (0, 0)),
            pl.BlockSpec((wa + wl, tn), lambda i, j: (0, j)),
            pl.BlockSpec((tm, tn), lambda i, j: (i, j)),
        ],
        out_specs=pl.BlockSpec((tm, tn), lambda i, j: (i, j)),
        out_shape=jax.ShapeDtypeStruct((m, d), F32),
        scratch_shapes=[pltpu.VMEM((tm, wa + wl), BF16)],
        compiler_params=_params("parallel", "arbitrary"),
        name="out_projection",
    )(attn, lru, g_attn, w_out, x2)


def _ffn_kernel(h_ref, gffn_ref, wg_ref, wu_ref, wd_ref, o_ref, hn_sc):
    @pl.when(pl.program_id(1) == 0)
    def _():
        rows = _tile(h_ref.shape[0], 128)
        for r in range(0, h_ref.shape[0], rows):
            h = h_ref[r:r + rows, :]
            hn_sc[r:r + rows, :] = (h * _rms_scale(h) * gffn_ref[...]).astype(BF16)
            o_ref[r:r + rows, :] = h

    hn = hn_sc[...]
    g = jnp.dot(hn, wg_ref[...], preferred_element_type=F32)
    u = jnp.dot(hn, wu_ref[...], preferred_element_type=F32)
    act = (g * (1.0 / (1.0 + jnp.exp(-g))) * u).astype(BF16)
    chunk = _tile(o_ref.shape[1], 1024)
    for c in range(0, o_ref.shape[1], chunk):
        o_ref[:, c:c + chunk] += jnp.dot(act, wd_ref[:, c:c + chunk], preferred_element_type=F32)


def _ffn(h, g_ffn, w_gate, w_up, w_down):
    m, d = h.shape
    dff = w_gate.shape[1]
    tm = _tile(m, 512)
    tf = _tile(dff, 256)
    return pl.pallas_call(
        _ffn_kernel,
        grid=(m // tm, dff // tf),
        in_specs=[
            pl.BlockSpec((tm, d), lambda i, f: (i, 0)),
            pl.BlockSpec((1, d), lambda i, f: (0, 0)),
            pl.BlockSpec((d, tf), lambda i, f: (0, f)),
            pl.BlockSpec((d, tf), lambda i, f: (0, f)),
            pl.BlockSpec((tf, d), lambda i, f: (f, 0)),
        ],
        out_specs=pl.BlockSpec((tm, d), lambda i, f: (i, 0)),
        out_shape=jax.ShapeDtypeStruct((m, d), F32),
        scratch_shapes=[pltpu.VMEM((tm, d), BF16)],
        compiler_params=_params("parallel", "arbitrary"),
        name="swiglu_ffn",
    )(h, g_ffn, w_gate, w_up, w_down)


def _layer(x, g_mix, w_in, b_f, g_q, g_k, conv_w, conv_b, w_rg, b_rg, w_ig, b_ig, lam,
           g_attn_out, g_lru_out, w_out, g_ffn, w_gate, w_up, w_down):
    batch, seq, d = x.shape
    n_heads = b_f.shape[0]
    head_dim = g_q.shape[0]
    attn_w = n_heads * head_dim
    lru_w = lam.shape[0]
    row = lambda v: v.reshape(1, -1).astype(F32)

    f_lo, f_hi = 3 * attn_w, 3 * attn_w + n_heads
    w_main, w_f = _split_w_in(jnp.swapaxes(w_in, 0, 1), f_lo=f_lo, n_f=n_heads)
    b_f_pad = jnp.pad(b_f.astype(F32), (0, V7X_LANES - n_heads)).reshape(1, V7X_LANES)
    qk_gain = jnp.concatenate([jnp.tile(g_q.astype(F32) * (head_dim ** -0.5 * LOG2E), n_heads),
                               jnp.tile(g_k.astype(F32), n_heads),
                               jnp.ones((w_main.shape[1] - 2 * attn_w,), F32)]).reshape(1, -1)
    w_gates = jnp.concatenate([w_rg, w_ig], axis=-1).astype(BF16)

    x2 = x.reshape(batch * seq, d)
    proj, f_logit = _in_projection(x2, row(g_mix), w_main, w_f, qk_gain,
                                   n_norm_cols=2 * attn_w, head_dim=head_dim)
    c_col = _forget_cumsum(f_logit.reshape(batch, seq, V7X_LANES), b_f_pad)
    qk_bound = (head_dim * BF16_ROUNDING_SLACK) * jnp.max(jnp.abs(qk_gain[:, :attn_w])) * jnp.max(jnp.abs(g_k.astype(F32)))
    qk_bound = jnp.full((1, V7X_LANES), qk_bound, F32)
    attn, w_out16, w_down16 = _attention(
        proj, c_col, qk_bound, [w_out.astype(F32), w_down.astype(F32)],
        batch=batch, seq=seq, n_heads=n_heads, head_dim=head_dim)
    lru, w_gate16, w_up16 = _rglru(
        proj, conv_w.astype(F32), row(conv_b), w_gates, row(b_rg), row(b_ig), row(lam), row(g_lru_out),
        [w_gate.astype(F32), w_up.astype(F32)], batch=batch, seq=seq, width=lru_w, ux_col=3 * attn_w)
    h = _out_projection(attn, lru, row(g_attn_out), w_out16, x2)
    out = _ffn(h, row(g_ffn), w_gate16, w_up16, w_down16)
    return out.reshape(batch, seq, d)


def kernel(x, g_mix, w_in, b_f, g_q, g_k, conv_w, conv_b, w_rg, b_rg, w_ig, b_ig, lam, g_attn_out, g_lru_out,
           w_out, g_ffn, w_gate, w_up, w_down):
    h = x
    for l in range(g_mix.shape[0]):
        h = _layer(h, g_mix[l], w_in[l], b_f[l], g_q[l], g_k[l], conv_w[l], conv_b[l], w_rg[l], b_rg[l],
                   w_ig[l], b_ig[l], lam[l], g_attn_out[l], g_lru_out[l], w_out[l], g_ffn[l], w_gate[l],
                   w_up[l], w_down[l])
    return h
```

```python
import functools
import math

import jax
import jax.numpy as jnp
from jax import lax
from jax.experimental import pallas as pl
from jax.experimental.pallas import tpu as pltpu

EPS = 1e-6
RG_C = 8.0
LOG2E = math.log2(math.e)
EXP2_ZERO = 160.0
BF16_ROUNDING_SLACK = (1.0 + 2.0 ** -8) ** 2 * 1.001
CONV_WIDTH = 4

V7X_LANES = 128
V7X_SUBLANES = 8
V7X_VMEM_LIMIT_BYTES = 56 * 1024 * 1024

F32 = jnp.float32
BF16 = jnp.bfloat16


def _tile(dim, pref):
    t = min(dim, pref)
    while dim % t:
        t //= 2
    return t


def _params(*sem):
    return pltpu.CompilerParams(dimension_semantics=sem, vmem_limit_bytes=V7X_VMEM_LIMIT_BYTES)


def _rms_scale(x):
    return lax.rsqrt(jnp.mean(x * x, axis=-1, keepdims=True) + EPS)


def _sigmoid(z):
    return 0.5 * jnp.tanh(0.5 * z) + 0.5


def _split_w_in_kernel(wt_ref, ft_ref, main_ref, f_ref, *, n_f):
    main_ref[...] = wt_ref[...].T.astype(BF16)

    @pl.when(pl.program_id(0) == 0)
    def _():
        lane = lax.broadcasted_iota(jnp.int32, f_ref.shape, 1)
        f_ref[...] = jnp.where(lane < n_f, ft_ref[...].T, 0.0).astype(BF16)


def _split_w_in(w_in_t, *, f_lo, n_f):
    n_in, d = w_in_t.shape
    n_main = n_in - n_f
    cols = _tile(math.gcd(f_lo, n_main - f_lo), 256)
    assert cols % V7X_LANES == 0 and f_lo + V7X_LANES <= n_in and n_f % V7X_SUBLANES == 0
    return pl.pallas_call(
        functools.partial(_split_w_in_kernel, n_f=n_f),
        grid=(n_main // cols,),
        in_specs=[
            pl.BlockSpec((pl.Element(cols), pl.Element(d)),
                         lambda j: (pl.multiple_of(jnp.where(j * cols < f_lo, j * cols, j * cols + n_f),
                                                   V7X_SUBLANES), 0)),
            pl.BlockSpec((pl.Element(V7X_LANES), pl.Element(d)), lambda j: (f_lo, 0)),
        ],
        out_specs=[pl.BlockSpec((d, cols), lambda j: (0, j)),
                   pl.BlockSpec((d, V7X_LANES), lambda j: (0, 0))],
        out_shape=[jax.ShapeDtypeStruct((d, n_main), BF16),
                   jax.ShapeDtypeStruct((d, V7X_LANES), BF16)],
        compiler_params=_params("arbitrary"),
        name="split_w_in",
    )(w_in_t, w_in_t)


def _inproj_kernel(x_ref, gmix_ref, w_ref, wf_ref, gain_ref, proj_ref, f_ref, xn_sc, *, n_norm_tiles, head_dim):
    j = pl.program_id(1)

    @pl.when(j == 0)
    def _():
        x = x_ref[...]
        xn = (x * _rms_scale(x) * gmix_ref[...]).astype(BF16)
        xn_sc[...] = xn
        f_ref[...] = jnp.dot(xn, wf_ref[...], preferred_element_type=F32)

    @pl.when(j < n_norm_tiles)
    def _():
        step = 2 * head_dim
        for c0 in range(0, proj_ref.shape[1], step):
            y = jnp.dot(xn_sc[...], w_ref[:, c0:c0 + step], preferred_element_type=F32)
            for c in range(c0, c0 + step, head_dim):
                ys = y[:, c - c0:c - c0 + head_dim]
                proj_ref[:, c:c + head_dim] = (ys * _rms_scale(ys) * gain_ref[:, c:c + head_dim]).astype(BF16)

    @pl.when(j >= n_norm_tiles)
    def _():
        proj_ref[...] = jnp.dot(xn_sc[...], w_ref[...], preferred_element_type=F32).astype(BF16)


def _in_projection(x2, g_mix, w_main, w_f, gain, *, n_norm_cols, head_dim):
    m, d = x2.shape
    n = w_main.shape[1]
    tm = _tile(m, 512)
    tn = _tile(math.gcd(n, n_norm_cols), 1024)
    kern = functools.partial(_inproj_kernel, n_norm_tiles=n_norm_cols // tn, head_dim=head_dim)
    return pl.pallas_call(
        kern,
        grid=(m // tm, n // tn),
        in_specs=[
            pl.BlockSpec((tm, d), lambda i, j: (i, 0)),
            pl.BlockSpec((1, d), lambda i, j: (0, 0)),
            pl.BlockSpec((d, tn), lambda i, j: (0, j)),
            pl.BlockSpec((d, V7X_LANES), lambda i, j: (0, 0)),
            pl.BlockSpec((1, tn), lambda i, j: (0, j)),
        ],
        out_specs=[
            pl.BlockSpec((tm, tn), lambda i, j: (i, j)),
            pl.BlockSpec((tm, V7X_LANES), lambda i, j: (i, 0)),
        ],
        out_shape=[
            jax.ShapeDtypeStruct((m, n), BF16),
            jax.ShapeDtypeStruct((m, V7X_LANES), F32),
        ],
        scratch_shapes=[pltpu.VMEM((tm, d), BF16)],
        compiler_params=_params("parallel", "arbitrary"),
        name="in_projection",
    )(x2, g_mix, w_main, w_f, gain)


def _forget_cumsum_kernel(f_ref, bf_ref, c_ref):
    z = f_ref[0] + bf_ref[...]
    c = jnp.minimum(z, 0.0) - jnp.log1p(jnp.exp(-jnp.abs(z)))
    row = lax.broadcasted_iota(jnp.int32, c.shape, 0)
    d = 1
    while d < c.shape[0]:
        c = c + jnp.where(row >= d, pltpu.roll(c, d, axis=0), 0.0)
        d *= 2
    c_ref[0] = c


def _forget_cumsum(f3, b_f_pad):
    b, s, lanes = f3.shape
    return pl.pallas_call(
        _forget_cumsum_kernel,
        grid=(b,),
        in_specs=[
            pl.BlockSpec((1, s, lanes), lambda i: (i, 0, 0)),
            pl.BlockSpec((1, lanes), lambda i: (0, 0)),
        ],
        out_specs=pl.BlockSpec((1, s, lanes), lambda i: (i, 0, 0)),
        out_shape=jax.ShapeDtypeStruct((b, s, lanes), F32),
        compiler_params=_params("parallel"),
        name="forget_cumsum",
    )(f3, b_f_pad)


def _bias_columns(c, ones_first):
    hi = c.astype(BF16).astype(F32)
    mid = (c - hi).astype(BF16).astype(F32)
    lo = (c - hi - mid).astype(BF16).astype(F32)
    lane = lax.broadcasted_iota(jnp.int32, (c.shape[0], V7X_LANES), 1)
    first_c, first_one = (3, 0) if ones_first else (0, 3)
    cols = jnp.where(lane == first_c, hi, jnp.where(lane == first_c + 1, mid, jnp.where(lane == first_c + 2, lo, 0.0)))
    ones = (lane >= first_one) & (lane < first_one + 3)
    return jnp.where(ones, 1.0, cols).astype(BF16)


def _attn_kernel(*refs, tile, chunk, n_cast):
    q_ref, k_ref, v_ref, c_ref, bound_ref = refs[:5]
    w32_refs = refs[5:5 + n_cast]
    o_ref = refs[5 + n_cast]
    w16_refs = refs[6 + n_cast:6 + 2 * n_cast]
    qa_sc, ka_sc, va_sc, m_sc, acc_sc, s0_sc, s1_sc, s2_sc = refs[6 + 2 * n_cast:]

    h = pl.program_id(1)
    qi = pl.program_id(2)
    seq, hd = k_ref.shape
    tq = 2 * tile
    lanes = m_sc.shape[1]

    def head_c(r0, n):
        cc = c_ref[0, pl.ds(r0, n), :]
        lane = lax.broadcasted_iota(jnp.int32, cc.shape, 1)
        return LOG2E * jnp.sum(jnp.where(lane == h, cc, 0.0), axis=1, keepdims=True)

    @pl.when(qi == 0)
    def _():
        def fill(ci, carry):
            r0 = pl.multiple_of(ci * chunk, chunk)
            ka_sc[pl.ds(r0, chunk), :hd] = k_ref[pl.ds(r0, chunk), :]
            ka_sc[pl.ds(r0, chunk), hd:] = _bias_columns(-head_c(r0, chunk), ones_first=True)
            va_sc[pl.ds(r0, chunk), :hd] = v_ref[pl.ds(r0, chunk), :]
            va_sc[pl.ds(r0, chunk), hd:] = jnp.ones((chunk, va_sc.shape[1] - hd), BF16)
            return carry
        lax.fori_loop(0, seq // chunk, fill, 0)

    qa_sc[:, :hd] = q_ref[...]
    qa_sc[:, hd:] = _bias_columns(head_c(pl.multiple_of(qi * tq, tq), tq), ones_first=False)
    m_sc[...] = jnp.full(m_sc.shape, -jnp.inf, F32)
    acc_sc[...] = jnp.zeros(acc_sc.shape, F32)

    def logits(sub, kb):
        k0 = pl.multiple_of(kb * tile, tile)
        return lax.dot_general(qa_sc[sub * tile:(sub + 1) * tile, :], ka_sc[pl.ds(k0, tile), :],
                               (((1,), (1,)), ((), ())), preferred_element_type=F32)

    def update(sub, *blocks):
        rows = slice(sub * tile, (sub + 1) * tile)
        m_prev = m_sc[rows, :]
        m_new = m_prev
        logit_blocks = []
        for s, kb, masked in blocks:
            if masked:
                r = lax.broadcasted_iota(jnp.int32, s.shape, 0)
                c = lax.broadcasted_iota(jnp.int32, s.shape, 1)
                s = jnp.where(c <= r, s, -jnp.inf)
            logit_blocks.append(s)
            m_new = jnp.maximum(m_new, jnp.max(s, axis=1, keepdims=True))
        m_wide = jnp.tile(m_new, (1, tile // lanes))
        pv = None
        for s, (_, kb, _) in zip(logit_blocks, blocks):
            k0 = pl.multiple_of(kb * tile, tile)
            term = jnp.dot(jnp.exp2(s - m_wide).astype(BF16), va_sc[pl.ds(k0, tile), :],
                           preferred_element_type=F32)
            pv = term if pv is None else pv + term
        alpha = jnp.exp2(m_prev - m_new)
        acc_sc[rows, :] = jnp.tile(alpha, (1, acc_sc.shape[1] // lanes)) * acc_sc[rows, :] + pv
        m_sc[rows, :] = m_new

    kd = 2 * qi
    n_blocks = seq // tile
    c_end = c_ref[0, pl.ds(tile - 1, n_blocks, stride=tile), :]
    lane = lax.broadcasted_iota(jnp.int32, c_end.shape, 1)
    c_end = LOG2E * jnp.sum(jnp.where(lane == h, c_end, 0.0), axis=1, keepdims=True)
    c_start = head_c(pl.multiple_of(qi * tq, tq), V7X_SUBLANES)[0:1, :]
    kb_idx = lax.broadcasted_iota(jnp.int32, c_end.shape, 0)
    dead = (2.0 * bound_ref[0:1, 0:1] + (c_start - c_end) <= -EXP2_ZERO) & (kb_idx < kd)
    kb0 = 2 * (jnp.sum(dead.astype(jnp.int32)) // 2)
    pairs = qi - kb0 // 2

    for sub in range(2):
        s0_sc[sub] = logits(sub, kb0)
    s2_sc[...] = logits(1, kd + 1)

    def block_pair(kb):
        for sub in range(2):
            s1_sc[sub] = logits(sub, kb + 1)
        for sub in range(2):
            update(sub, (s0_sc[sub], kb, False))
        for sub in range(2):
            s0_sc[sub] = logits(sub, kb + 2)
        for sub in range(2):
            update(sub, (s1_sc[sub], kb + 1, False))

    def two_pairs(it, carry):
        block_pair(kb0 + 4 * it)
        block_pair(kb0 + 4 * it + 2)
        return carry

    def one_pair(it, carry):
        block_pair(kb0 + 4 * (pairs // 2))
        return carry

    lax.fori_loop(0, pairs // 2, two_pairs, 0)
    lax.fori_loop(0, pairs % 2, one_pair, 0)
    update(0, (s0_sc[0], kd, True))
    update(1, (s0_sc[1], kd, False), (s2_sc[...], kd + 1, True))
    o_ref[...] = acc_sc[:, :hd] / acc_sc[:, hd:2 * hd]
    for w32_ref, w16_ref in zip(w32_refs, w16_refs):
        w16_ref[...] = w32_ref[...].astype(BF16)


def _cast_block_spec(rows, cols, n_steps, step_of):
    bf16_rows = 2 * V7X_SUBLANES
    br = next(r for r in range(bf16_rows, rows + 1, bf16_rows) if rows % r == 0 and rows // r <= n_steps)
    last = rows // br - 1
    return pl.BlockSpec((br, cols), lambda *idx: (jnp.minimum(step_of(*idx), last), 0))


def _attention(proj, c_col, qk_bound, weights_f32, *, batch, seq, n_heads, head_dim):
    assert head_dim == V7X_LANES
    tile = _tile(seq, 512)
    tq = 2 * tile
    assert seq % tq == 0
    nq = seq // tq
    attn_w = n_heads * head_dim
    n_steps = batch * n_heads * nq
    step_of = lambda b, h, i: (b * n_heads + h) * nq + i
    cast_specs = [_cast_block_spec(*w.shape, n_steps, step_of) for w in weights_f32]
    kern = functools.partial(_attn_kernel, tile=tile, chunk=tile, n_cast=len(weights_f32))
    return pl.pallas_call(
        kern,
        grid=(batch, n_heads, nq),
        in_specs=[
            pl.BlockSpec((tq, head_dim), lambda b, h, i: (b * nq + i, h)),
            pl.BlockSpec((seq, head_dim), lambda b, h, i: (b, n_heads + h)),
            pl.BlockSpec((seq, head_dim), lambda b, h, i: (b, 2 * n_heads + h)),
            pl.BlockSpec((1, seq, V7X_LANES), lambda b, h, i: (b, 0, 0)),
            pl.BlockSpec((1, V7X_LANES), lambda b, h, i: (0, 0)),
        ] + cast_specs,
        out_specs=[pl.BlockSpec((tq, head_dim), lambda b, h, i: (b * nq + i, h))] + cast_specs,
        out_shape=[jax.ShapeDtypeStruct((batch * seq, attn_w), F32)]
        + [jax.ShapeDtypeStruct(w.shape, BF16) for w in weights_f32],
        scratch_shapes=[
            pltpu.VMEM((tq, 2 * head_dim), BF16),
            pltpu.VMEM((seq, 2 * head_dim), BF16),
            pltpu.VMEM((seq, 2 * head_dim), BF16),
            pltpu.VMEM((tq, V7X_LANES), F32),
            pltpu.VMEM((tq, 2 * head_dim), F32),
            pltpu.VMEM((2, tile, tile), F32),
            pltpu.VMEM((2, tile, tile), F32),
            pltpu.VMEM((tile, tile), F32),
        ],
        compiler_params=_params("parallel", "parallel", "arbitrary"),
        name="fox_attention",
    )(proj, proj, proj, c_col, qk_bound, *weights_f32)


def _lru_kernel(*refs, block, n_cast):
    ux_ref, uy_ref, cw_ref, cb_ref, wg_ref, brg_ref, big_ref, lam_ref, gout_ref = refs[:9]
    w32_refs = refs[9:9 + n_cast]
    o_ref = refs[9 + n_cast]
    w16_refs = refs[10 + n_cast:10 + 2 * n_cast]
    tail_sc, hlast_sc, xr_sc, a_sc, b_sc, shift_sc = refs[10 + 2 * n_cast:]
    for w32_ref, w16_ref in zip(w32_refs, w16_refs):
        w16_ref[...] = w32_ref[...].astype(BF16)

    t = pl.program_id(1)
    rows, width = xr_sc.shape
    sub = V7X_SUBLANES

    @pl.when(t == 0)
    def _():
        tail_sc[...] = jnp.zeros(tail_sc.shape, F32)
        hlast_sc[...] = jnp.zeros(hlast_sc.shape, F32)
        t_out = lax.broadcasted_iota(jnp.int32, (rows, rows), 0)
        t_in = lax.broadcasted_iota(jnp.int32, (rows, rows), 1)
        for d in range(1, CONV_WIDTH):
            shift_sc[d - 1] = (t_in == t_out - d).astype(BF16)

    u16 = ux_ref[...]
    u = u16.astype(F32)
    cw = cw_ref[...]
    cb = cb_ref[...]
    shifted_u = jnp.dot(shift_sc[...].reshape((CONV_WIDTH - 1) * rows, rows), u16,
                        preferred_element_type=F32)
    xr = cw[CONV_WIDTH - 1:CONV_WIDTH, :] * u + cb
    for d in range(1, CONV_WIDTH):
        xr = xr + cw[CONV_WIDTH - 1 - d:CONV_WIDTH - d, :] * shifted_u[(d - 1) * rows:d * rows, :]
    xr_sc[...] = xr
    u_head = u[0:sub, :]
    tail = tail_sc[...]
    row8 = lax.broadcasted_iota(jnp.int32, (sub, width), 0)
    xr_head = cw[CONV_WIDTH - 1:CONV_WIDTH, :] * u_head + cb
    for d in range(1, CONV_WIDTH):
        shifted = jnp.where(row8 < d, pltpu.roll(tail, d, axis=0), pltpu.roll(u_head, d, axis=0))
        xr_head = xr_head + cw[CONV_WIDTH - 1 - d:CONV_WIDTH - d, :] * shifted
    xr_sc[0:sub, :] = xr_head
    tail_sc[...] = u[rows - sub:rows, :]

    lam = lam_ref[...]
    log2_a_max = (-RG_C * LOG2E) * (jnp.maximum(-lam, 0.0) + jnp.log1p(jnp.exp(-jnp.abs(lam))))

    sub_idx = lax.broadcasted_iota(jnp.int32, (rows // sub, sub, block), 1)
    for n in range(width // block):
        cs = slice(n * block, (n + 1) * block)
        xb = xr_sc[:, cs]
        g = jnp.dot(xb.astype(BF16), wg_ref[n], preferred_element_type=F32)
        r = _sigmoid(g[:, :block] + brg_ref[:, cs])
        i = _sigmoid(g[:, block:] + big_ref[:, cs])
        a = jnp.exp2(r * log2_a_max[:, cs])
        b = jnp.sqrt(1.0 - a * a) * (i * xb)
        a3 = a.reshape(rows // sub, sub, block)
        b3 = b.reshape(rows // sub, sub, block)
        d = 1
        while d < sub:
            ok = sub_idx >= d
            b3 = jnp.where(ok, a3 * pltpu.roll(b3, d, axis=1) + b3, b3)
            a3 = jnp.where(ok, a3 * pltpu.roll(a3, d, axis=1), a3)
            d *= 2
        a_sc[:, cs] = a3.reshape(rows, block)
        b_sc[:, cs] = b3.reshape(rows, block)

    def group(gi, h_prev):
        r0 = pl.multiple_of(gi * sub, sub)
        h8 = a_sc[pl.ds(r0, sub), :] * h_prev + b_sc[pl.ds(r0, sub), :]
        b_sc[pl.ds(r0, sub), :] = h8
        return jnp.broadcast_to(h8[sub - 1:sub, :], h8.shape)

    hlast_sc[...] = lax.fori_loop(0, rows // sub, group, hlast_sc[...])

    uy = uy_ref[...].astype(F32)
    gelu = 0.5 * uy * (1.0 + jnp.tanh(math.sqrt(2.0 / math.pi) * (uy + 0.044715 * (uy * uy * uy))))
    y = gelu * b_sc[...]
    o_ref[...] = (y * _rms_scale(y) * gout_ref[...]).astype(BF16)


def _rglru(proj, conv_w, conv_b, w_gates, b_rg, b_ig, lam, g_out, weights_f32, *, batch, seq, width, ux_col):
    nb, block, _ = w_gates.shape
    rows = _tile(seq, 256)
    nt = seq // rows
    ux_blk = ux_col // width
    vec = lambda: pl.BlockSpec((1, width), lambda b, t: (0, 0))
    cast_specs = [_cast_block_spec(*w.shape, batch * nt, lambda b, t: b * nt + t) for w in weights_f32]
    kern = functools.partial(_lru_kernel, block=block, n_cast=len(weights_f32))
    return pl.pallas_call(
        kern,
        grid=(batch, nt),
        in_specs=[
            pl.BlockSpec((rows, width), lambda b, t: (b * nt + t, ux_blk)),
            pl.BlockSpec((rows, width), lambda b, t: (b * nt + t, ux_blk + 1)),
            pl.BlockSpec((CONV_WIDTH, width), lambda b, t: (0, 0)),
            vec(),
            pl.BlockSpec((nb, block, 2 * block), lambda b, t: (0, 0, 0)),
            vec(), vec(), vec(), vec(),
        ] + cast_specs,
        out_specs=[pl.BlockSpec((rows, width), lambda b, t: (b * nt + t, 0))] + cast_specs,
        out_shape=[jax.ShapeDtypeStruct((batch * seq, width), BF16)]
        + [jax.ShapeDtypeStruct(w.shape, BF16) for w in weights_f32],
        scratch_shapes=[
            pltpu.VMEM((V7X_SUBLANES, width), F32),
            pltpu.VMEM((V7X_SUBLANES, width), F32),
            pltpu.VMEM((rows, width), F32),
            pltpu.VMEM((rows, width), F32),
            pltpu.VMEM((rows, width), F32),
            pltpu.VMEM((CONV_WIDTH - 1, rows, rows), BF16),
        ],
        compiler_params=_params("parallel", "arbitrary"),
        name="rglru",
    )(proj, proj, conv_w, conv_b, w_gates, b_rg, b_ig, lam, g_out, *weights_f32)


def _outproj_kernel(attn_ref, lru_ref, gattn_ref, w_ref, x_ref, h_ref, mixed_sc):
    @pl.when(pl.program_id(1) == 0)
    def _():
        wa = attn_ref.shape[1]
        rows = _tile(attn_ref.shape[0], 256)
        for r in range(0, attn_ref.shape[0], rows):
            a = attn_ref[r:r + rows, :]
            mixed_sc[r:r + rows, :wa] = (a * _rms_scale(a) * gattn_ref[...]).astype(BF16)
        mixed_sc[:, wa:] = lru_ref[...]

    chunk = _tile(h_ref.shape[1], 256)
    for c in range(0, h_ref.shape[1], chunk):
        h_ref[:, c:c + chunk] = x_ref[:, c:c + chunk] + jnp.dot(mixed_sc[...], w_ref[:, c:c + chunk],
                                                                 preferred_element_type=F32)


def _out_projection(attn, lru, g_attn, w_out, x2):
    m, wa = attn.shape
    wl = lru.shape[1]
    d = w_out.shape[1]
    tm = _tile(m, 1024)
    tn = _tile(d, 512)
    return pl.pallas_call(
        _outproj_kernel,
        grid=(m // tm, d // tn),
        in_specs=[
            pl.BlockSpec((tm, wa), lambda i, j: (i, 0)),
            pl.BlockSpec((tm, wl), lambda i, j: (i, 0)),
            pl.BlockSpec((1, wa), lambda i, j: (0, 0)),
            pl.BlockSpec((wa + wl, tn), lambda i, j: (0, j)),
            pl.BlockSpec((tm, tn), lambda i, j: (i, j)),
        ],
        out_specs=pl.BlockSpec((tm, tn), lambda i, j: (i, j)),
        out_shape=jax.ShapeDtypeStruct((m, d), F32),
        scratch_shapes=[pltpu.VMEM((tm, wa + wl), BF16)],
        compiler_params=_params("parallel", "arbitrary"),
        name="out_projection",
    )(attn, lru, g_attn, w_out, x2)


def _ffn_kernel(h_ref, gffn_ref, wg_ref, wu_ref, wd_ref, o_ref, hn_sc):
    @pl.when(pl.program_id(1) == 0)
    def _():
        rows = _tile(h_ref.shape[0], 128)
        for r in range(0, h_ref.shape[0], rows):
            h = h_ref[r:r + rows, :]
            hn_sc[r:r + rows, :] = (h * _rms_scale(h) * gffn_ref[...]).astype(BF16)
            o_ref[r:r + rows, :] = h

    hn = hn_sc[...]
    g = jnp.dot(hn, wg_ref[...], preferred_element_type=F32)
    u = jnp.dot(hn, wu_ref[...], preferred_element_type=F32)
    act = (g * (1.0 / (1.0 + jnp.exp(-g))) * u).astype(BF16)
    chunk = _tile(o_ref.shape[1], 1024)
    for c in range(0, o_ref.shape[1], chunk):
        o_ref[:, c:c + chunk] += jnp.dot(act, wd_ref[:, c:c + chunk], preferred_element_type=F32)


def _ffn(h, g_ffn, w_gate, w_up, w_down):
    m, d = h.shape
    dff = w_gate.shape[1]
    tm = _tile(m, 512)
    tf = _tile(dff, 256)
    return pl.pallas_call(
        _ffn_kernel,
        grid=(m // tm, dff // tf),
        in_specs=[
            pl.BlockSpec((tm, d), lambda i, f: (i, 0)),
            pl.BlockSpec((1, d), lambda i, f: (0, 0)),
            pl.BlockSpec((d, tf), lambda i, f: (0, f)),
            pl.BlockSpec((d, tf), lambda i, f: (0, f)),
            pl.BlockSpec((tf, d), lambda i, f: (f, 0)),
        ],
        out_specs=pl.BlockSpec((tm, d), lambda i, f: (i, 0)),
        out_shape=jax.ShapeDtypeStruct((m, d), F32),
        scratch_shapes=[pltpu.VMEM((tm, d), BF16)],
        compiler_params=_params("parallel", "arbitrary"),
        name="swiglu_ffn",
    )(h, g_ffn, w_gate, w_up, w_down)


def _layer(x, g_mix, w_in, b_f, g_q, g_k, conv_w, conv_b, w_rg, b_rg, w_ig, b_ig, lam,
           g_attn_out, g_lru_out, w_out, g_ffn, w_gate, w_up, w_down):
    batch, seq, d = x.shape
    n_heads = b_f.shape[0]
    head_dim = g_q.shape[0]
    attn_w = n_heads * head_dim
    lru_w = lam.shape[0]
    row = lambda v: v.reshape(1, -1).astype(F32)

    f_lo, f_hi = 3 * attn_w, 3 * attn_w + n_heads
    w_main, w_f = _split_w_in(jnp.swapaxes(w_in, 0, 1), f_lo=f_lo, n_f=n_heads)
    b_f_pad = jnp.pad(b_f.astype(F32), (0, V7X_LANES - n_heads)).reshape(1, V7X_LANES)
    qk_gain = jnp.concatenate([jnp.tile(g_q.astype(F32) * (head_dim ** -0.5 * LOG2E), n_heads),
                               jnp.tile(g_k.astype(F32), n_heads),
                               jnp.ones((w_main.shape[1] - 2 * attn_w,), F32)]).reshape(1, -1)
    w_gates = jnp.concatenate([w_rg, w_ig], axis=-1).astype(BF16)

    x2 = x.reshape(batch * seq, d)
    proj, f_logit = _in_projection(x2, row(g_mix), w_main, w_f, qk_gain,
                                   n_norm_cols=2 * attn_w, head_dim=head_dim)
    c_col = _forget_cumsum(f_logit.reshape(batch, seq, V7X_LANES), b_f_pad)
    qk_bound = (head_dim * BF16_ROUNDING_SLACK) * jnp.max(jnp.abs(qk_gain[:, :attn_w])) * jnp.max(jnp.abs(g_k.astype(F32)))
    qk_bound = jnp.full((1, V7X_LANES), qk_bound, F32)
    attn, w_out16, w_down16 = _attention(
        proj, c_col, qk_bound, [w_out.astype(F32), w_down.astype(F32)],
        batch=batch, seq=seq, n_heads=n_heads, head_dim=head_dim)
    lru, w_gate16, w_up16 = _rglru(
        proj, conv_w.astype(F32), row(conv_b), w_gates, row(b_rg), row(b_ig), row(lam), row(g_lru_out),
        [w_gate.astype(F32), w_up.astype(F32)], batch=batch, seq=seq, width=lru_w, ux_col=3 * attn_w)
    h = _out_projection(attn, lru, row(g_attn_out), w_out16, x2)
    out = _ffn(h, row(g_ffn), w_gate16, w_up16, w_down16)
    return out.reshape(batch, seq, d)


def kernel(x, g_mix, w_in, b_f, g_q, g_k, conv_w, conv_b, w_rg, b_rg, w_ig, b_ig, lam, g_attn_out, g_lru_out,
           w_out, g_ffn, w_gate, w_up, w_down):
    h = x
    for l in range(g_mix.shape[0]):
        h = _layer(h, g_mix[l], w_in[l], b_f[l], g_q[l], g_k[l], conv_w[l], conv_b[l], w_rg[l], b_rg[l],
                   w_ig[l], b_ig[l], lam[l], g_attn_out[l], g_lru_out[l], w_out[l], g_ffn[l], w_gate[l],
                   w_up[l], w_down[l])
    return h
```

```python
import functools
import math

import jax
import jax.numpy as jnp
from jax import lax
from jax.experimental import pallas as pl
from jax.experimental.pallas import tpu as pltpu

EPS = 1e-6
RG_C = 8.0
LOG2E = math.log2(math.e)
EXP2_ZERO = 160.0
BF16_ROUNDING_SLACK = (1.0 + 2.0 ** -8) ** 2 * 1.001
CONV_WIDTH = 4

V7X_LANES = 128
V7X_SUBLANES = 8
V7X_VMEM_LIMIT_BYTES = 56 * 1024 * 1024

F32 = jnp.float32
BF16 = jnp.bfloat16


def _tile(dim, pref):
    t = min(dim, pref)
    while dim % t:
        t //= 2
    return t


def _params(*sem):
    return pltpu.CompilerParams(dimension_semantics=sem, vmem_limit_bytes=V7X_VMEM_LIMIT_BYTES)


def _rms_scale(x):
    return lax.rsqrt(jnp.mean(x * x, axis=-1, keepdims=True) + EPS)


def _sigmoid(z):
    return 0.5 * jnp.tanh(0.5 * z) + 0.5


def _split_w_in_kernel(wt_ref, ft_ref, main_ref, f_ref, *, n_f):
    main_ref[...] = wt_ref[...].T.astype(BF16)

    @pl.when(pl.program_id(0) == 0)
    def _():
        lane = lax.broadcasted_iota(jnp.int32, f_ref.shape, 1)
        f_ref[...] = jnp.where(lane < n_f, ft_ref[...].T, 0.0).astype(BF16)


def _split_w_in(w_in_t, *, f_lo, n_f):
    n_in, d = w_in_t.shape
    n_main = n_in - n_f
    cols = _tile(math.gcd(f_lo, n_main - f_lo), 256)
    assert cols % V7X_LANES == 0 and f_lo + V7X_LANES <= n_in and n_f % V7X_SUBLANES == 0
    return pl.pallas_call(
        functools.partial(_split_w_in_kernel, n_f=n_f),
        grid=(n_main // cols,),
        in_specs=[
            pl.BlockSpec((pl.Element(cols), pl.Element(d)),
                         lambda j: (pl.multiple_of(jnp.where(j * cols < f_lo, j * cols, j * cols + n_f),
                                                   V7X_SUBLANES), 0)),
            pl.BlockSpec((pl.Element(V7X_LANES), pl.Element(d)), lambda j: (f_lo, 0)),
        ],
        out_specs=[pl.BlockSpec((d, cols), lambda j: (0, j)),
                   pl.BlockSpec((d, V7X_LANES), lambda j: (0, 0))],
        out_shape=[jax.ShapeDtypeStruct((d, n_main), BF16),
                   jax.ShapeDtypeStruct((d, V7X_LANES), BF16)],
        compiler_params=_params("arbitrary"),
        name="split_w_in",
    )(w_in_t, w_in_t)


def _inproj_kernel(x_ref, gmix_ref, w_ref, wf_ref, gain_ref, proj_ref, f_ref, xn_sc, *, n_norm_tiles, head_dim):
    j = pl.program_id(1)

    @pl.when(j == 0)
    def _():
        x = x_ref[...]
        xn = (x * _rms_scale(x) * gmix_ref[...]).astype(BF16)
        xn_sc[...] = xn
        f_ref[...] = jnp.dot(xn, wf_ref[...], preferred_element_type=F32)

    @pl.when(j < n_norm_tiles)
    def _():
        step = 2 * head_dim
        for c0 in range(0, proj_ref.shape[1], step):
            y = jnp.dot(xn_sc[...], w_ref[:, c0:c0 + step], preferred_element_type=F32)
            for c in range(c0, c0 + step, head_dim):
                ys = y[:, c - c0:c - c0 + head_dim]
                proj_ref[:, c:c + head_dim] = (ys * _rms_scale(ys) * gain_ref[:, c:c + head_dim]).astype(BF16)

    @pl.when(j >= n_norm_tiles)
    def _():
        proj_ref[...] = jnp.dot(xn_sc[...], w_ref[...], preferred_element_type=F32).astype(BF16)


def _in_projection(x2, g_mix, w_main, w_f, gain, *, n_norm_cols, head_dim):
    m, d = x2.shape
    n = w_main.shape[1]
    tm = _tile(m, 512)
    tn = _tile(math.gcd(n, n_norm_cols), 1024)
    kern = functools.partial(_inproj_kernel, n_norm_tiles=n_norm_cols // tn, head_dim=head_dim)
    return pl.pallas_call(
        kern,
        grid=(m // tm, n // tn),
        in_specs=[
            pl.BlockSpec((tm, d), lambda i, j: (i, 0)),
            pl.BlockSpec((1, d), lambda i, j: (0, 0)),
            pl.BlockSpec((d, tn), lambda i, j: (0, j)),
            pl.BlockSpec((d, V7X_LANES), lambda i, j: (0, 0)),
            pl.BlockSpec((1, tn), lambda i, j: (0, j)),
        ],
        out_specs=[
            pl.BlockSpec((tm, tn), lambda i, j: (i, j)),
            pl.BlockSpec((tm, V7X_LANES), lambda i, j: (i, 0)),
        ],
        out_shape=[
            jax.ShapeDtypeStruct((m, n), BF16),
            jax.ShapeDtypeStruct((m, V7X_LANES), F32),
        ],
        scratch_shapes=[pltpu.VMEM((tm, d), BF16)],
        compiler_params=_params("parallel", "arbitrary"),
        name="in_projection",
    )(x2, g_mix, w_main, w_f, gain)


def _forget_cumsum_kernel(f_ref, bf_ref, c_ref):
    z = f_ref[0] + bf_ref[...]
    c = jnp.minimum(z, 0.0) - jnp.log1p(jnp.exp(-jnp.abs(z)))
    row = lax.broadcasted_iota(jnp.int32, c.shape, 0)
    d = 1
    while d < c.shape[0]:
        c = c + jnp.where(row >= d, pltpu.roll(c, d, axis=0), 0.0)
        d *= 2
    c_ref[0] = c


def _forget_cumsum(f3, b_f_pad):
    b, s, lanes = f3.shape
    return pl.pallas_call(
        _forget_cumsum_kernel,
        grid=(b,),
        in_specs=[
            pl.BlockSpec((1, s, lanes), lambda i: (i, 0, 0)),
            pl.BlockSpec((1, lanes), lambda i: (0, 0)),
        ],
        out_specs=pl.BlockSpec((1, s, lanes), lambda i: (i, 0, 0)),
        out_shape=jax.ShapeDtypeStruct((b, s, lanes), F32),
        compiler_params=_params("parallel"),
        name="forget_cumsum",
    )(f3, b_f_pad)


def _bias_columns(c, ones_first):
    hi = c.astype(BF16).astype(F32)
    mid = (c - hi).astype(BF16).astype(F32)
    lo = (c - hi - mid).astype(BF16).astype(F32)
    lane = lax.broadcasted_iota(jnp.int32, (c.shape[0], V7X_LANES), 1)
    first_c, first_one = (3, 0) if ones_first else (0, 3)
    cols = jnp.where(lane == first_c, hi, jnp.where(lane == first_c + 1, mid, jnp.where(lane == first_c + 2, lo, 0.0)))
    ones = (lane >= first_one) & (lane < first_one + 3)
    return jnp.where(ones, 1.0, cols).astype(BF16)


def _attn_kernel(*refs, tile, chunk, n_cast):
    q_ref, k_ref, v_ref, c_ref, bound_ref = refs[:5]
    w32_refs = refs[5:5 + n_cast]
    o_ref = refs[5 + n_cast]
    w16_refs = refs[6 + n_cast:6 + 2 * n_cast]
    qa_sc, ka_sc, va_sc, m_sc, acc_sc, s0_sc, s1_sc, s2_sc = refs[6 + 2 * n_cast:]

    h = pl.program_id(1)
    qi = pl.program_id(2)
    seq, hd = k_ref.shape
    tq = 2 * tile
    lanes = m_sc.shape[1]

    def head_c(r0, n):
        cc = c_ref[0, pl.ds(r0, n), :]
        lane = lax.broadcasted_iota(jnp.int32, cc.shape, 1)
        return LOG2E * jnp.sum(jnp.where(lane == h, cc, 0.0), axis=1, keepdims=True)

    @pl.when(qi == 0)
    def _():
        def fill(ci, carry):
            r0 = pl.multiple_of(ci * chunk, chunk)
            ka_sc[pl.ds(r0, chunk), :hd] = k_ref[pl.ds(r0, chunk), :]
            ka_sc[pl.ds(r0, chunk), hd:] = _bias_columns(-head_c(r0, chunk), ones_first=True)
            va_sc[pl.ds(r0, chunk), :hd] = v_ref[pl.ds(r0, chunk), :]
            va_sc[pl.ds(r0, chunk), hd:] = jnp.ones((chunk, va_sc.shape[1] - hd), BF16)
            return carry
        lax.fori_loop(0, seq // chunk, fill, 0)

    qa_sc[:, :hd] = q_ref[...]
    qa_sc[:, hd:] = _bias_columns(head_c(pl.multiple_of(qi * tq, tq), tq), ones_first=False)
    m_sc[...] = jnp.full(m_sc.shape, -jnp.inf, F32)
    acc_sc[...] = jnp.zeros(acc_sc.shape, F32)

    def logits(sub, kb):
        k0 = pl.multiple_of(kb * tile, tile)
        return lax.dot_general(qa_sc[sub * tile:(sub + 1) * tile, :], ka_sc[pl.ds(k0, tile), :],
                               (((1,), (1,)), ((), ())), preferred_element_type=F32)

    def update(sub, *blocks):
        rows = slice(sub * tile, (sub + 1) * tile)
        m_prev = m_sc[rows, :]
        m_new = m_prev
        logit_blocks = []
        for s, kb, masked in blocks:
            if masked:
                r = lax.broadcasted_iota(jnp.int32, s.shape, 0)
                c = lax.broadcasted_iota(jnp.int32, s.shape, 1)
                s = jnp.where(c <= r, s, -jnp.inf)
            logit_blocks.append(s)
            m_new = jnp.maximum(m_new, jnp.max(s, axis=1, keepdims=True))
        m_wide = jnp.tile(m_new, (1, tile // lanes))
        pv = None
        for s, (_, kb, _) in zip(logit_blocks, blocks):
            k0 = pl.multiple_of(kb * tile, tile)
            term = jnp.dot(jnp.exp2(s - m_wide).astype(BF16), va_sc[pl.ds(k0, tile), :],
                           preferred_element_type=F32)
            pv = term if pv is None else pv + term
        alpha = jnp.exp2(m_prev - m_new)
        acc_sc[rows, :] = jnp.tile(alpha, (1, acc_sc.shape[1] // lanes)) * acc_sc[rows, :] + pv
        m_sc[rows, :] = m_new

    kd = 2 * qi
    n_blocks = seq // tile
    c_end = c_ref[0, pl.ds(tile - 1, n_blocks, stride=tile), :]
    lane = lax.broadcasted_iota(jnp.int32, c_end.shape, 1)
    c_end = LOG2E * jnp.sum(jnp.where(lane == h, c_end, 0.0), axis=1, keepdims=True)
    c_start = head_c(pl.multiple_of(qi * tq, tq), V7X_SUBLANES)[0:1, :]
    kb_idx = lax.broadcasted_iota(jnp.int32, c_end.shape, 0)
    dead = (2.0 * bound_ref[0:1, 0:1] + (c_start - c_end) <= -EXP2_ZERO) & (kb_idx < kd)
    kb0 = 2 * (jnp.sum(dead.astype(jnp.int32)) // 2)
    pairs = qi - kb0 // 2

    for sub in range(2):
        s0_sc[sub] = logits(sub, kb0)
    s2_sc[...] = logits(1, kd + 1)

    def block_pair(kb):
        for sub in range(2):
            s1_sc[sub] = logits(sub, kb + 1)
        for sub in range(2):
            update(sub, (s0_sc[sub], kb, False))
        for sub in range(2):
            s0_sc[sub] = logits(sub, kb + 2)
        for sub in range(2):
            update(sub, (s1_sc[sub], kb + 1, False))

    def two_pairs(it, carry):
        block_pair(kb0 + 4 * it)
        block_pair(kb0 + 4 * it + 2)
        return carry

    def one_pair(it, carry):
        block_pair(kb0 + 4 * (pairs // 2))
        return carry

    lax.fori_loop(0, pairs // 2, two_pairs, 0)
    lax.fori_loop(0, pairs % 2, one_pair, 0)
    update(0, (s0_sc[0], kd, True))
    update(1, (s0_sc[1], kd, False), (s2_sc[...], kd + 1, True))
    o_ref[...] = acc_sc[:, :hd] / acc_sc[:, hd:2 * hd]
    for w32_ref, w16_ref in zip(w32_refs, w16_refs):
        w16_ref[...] = w32_ref[...].astype(BF16)


def _cast_block_spec(rows, cols, n_steps, step_of):
    bf16_rows = 2 * V7X_SUBLANES
    br = next(r for r in range(bf16_rows, rows + 1, bf16_rows) if rows % r == 0 and rows // r <= n_steps)
    last = rows // br - 1
    return pl.BlockSpec((br, cols), lambda *idx: (jnp.minimum(step_of(*idx), last), 0))


def _attention(proj, c_col, qk_bound, weights_f32, *, batch, seq, n_heads, head_dim):
    assert head_dim == V7X_LANES
    tile = _tile(seq, 512)
    tq = 2 * tile
    assert seq % tq == 0
    nq = seq // tq
    attn_w = n_heads * head_dim
    n_steps = batch * n_heads * nq
    step_of = lambda b, h, i: (b * n_heads + h) * nq + i
    cast_specs = [_cast_block_spec(*w.shape, n_steps, step_of) for w in weights_f32]
    kern = functools.partial(_attn_kernel, tile=tile, chunk=tile, n_cast=len(weights_f32))
    return pl.pallas_call(
        kern,
        grid=(batch, n_heads, nq),
        in_specs=[
            pl.BlockSpec((tq, head_dim), lambda b, h, i: (b * nq + i, h)),
            pl.BlockSpec((seq, head_dim), lambda b, h, i: (b, n_heads + h)),
            pl.BlockSpec((seq, head_dim), lambda b, h, i: (b, 2 * n_heads + h)),
            pl.BlockSpec((1, seq, V7X_LANES), lambda b, h, i: (b, 0, 0)),
            pl.BlockSpec((1, V7X_LANES), lambda b, h, i: (0, 0)),
        ] + cast_specs,
        out_specs=[pl.BlockSpec((tq, head_dim), lambda b, h, i: (b * nq + i, h))] + cast_specs,
        out_shape=[jax.ShapeDtypeStruct((batch * seq, attn_w), F32)]
        + [jax.ShapeDtypeStruct(w.shape, BF16) for w in weights_f32],
        scratch_shapes=[
            pltpu.VMEM((tq, 2 * head_dim), BF16),
            pltpu.VMEM((seq, 2 * head_dim), BF16),
            pltpu.VMEM((seq, 2 * head_dim), BF16),
            pltpu.VMEM((tq, V7X_LANES), F32),
            pltpu.VMEM((tq, 2 * head_dim), F32),
            pltpu.VMEM((2, tile, tile), F32),
            pltpu.VMEM((2, tile, tile), F32),
            pltpu.VMEM((tile, tile), F32),
        ],
        compiler_params=_params("parallel", "parallel", "arbitrary"),
        name="fox_attention",
    )(proj, proj, proj, c_col, qk_bound, *weights_f32)


def _lru_kernel(*refs, block, n_cast):
    ux_ref, uy_ref, cw_ref, cb_ref, wg_ref, brg_ref, big_ref, lam_ref, gout_ref = refs[:9]
    w32_refs = refs[9:9 + n_cast]
    o_ref = refs[9 + n_cast]
    w16_refs = refs[10 + n_cast:10 + 2 * n_cast]
    tail_sc, hlast_sc, xr_sc, a_sc, b_sc = refs[10 + 2 * n_cast:]
    for w32_ref, w16_ref in zip(w32_refs, w16_refs):
        w16_ref[...] = w32_ref[...].astype(BF16)

    t = pl.program_id(1)
    rows, width = xr_sc.shape
    sub = V7X_SUBLANES

    @pl.when(t == 0)
    def _():
        tail_sc[...] = jnp.zeros(tail_sc.shape, F32)
        hlast_sc[...] = jnp.zeros(hlast_sc.shape, F32)

    u = ux_ref[...].astype(F32)
    cw = cw_ref[...]
    cb = cb_ref[...]
    xr = cw[CONV_WIDTH - 1:CONV_WIDTH, :] * u + cb
    for d in range(1, CONV_WIDTH):
        xr = xr + cw[CONV_WIDTH - 1 - d:CONV_WIDTH - d, :] * pltpu.roll(u, d, axis=0)
    xr_sc[...] = xr
    u_head = u[0:sub, :]
    tail = tail_sc[...]
    row8 = lax.broadcasted_iota(jnp.int32, (sub, width), 0)
    xr_head = cw[CONV_WIDTH - 1:CONV_WIDTH, :] * u_head + cb
    for d in range(1, CONV_WIDTH):
        shifted = jnp.where(row8 < d, pltpu.roll(tail, d, axis=0), pltpu.roll(u_head, d, axis=0))
        xr_head = xr_head + cw[CONV_WIDTH - 1 - d:CONV_WIDTH - d, :] * shifted
    xr_sc[0:sub, :] = xr_head
    tail_sc[...] = u[rows - sub:rows, :]

    lam = lam_ref[...]
    log2_a_max = (-RG_C * LOG2E) * (jnp.maximum(-lam, 0.0) + jnp.log1p(jnp.exp(-jnp.abs(lam))))

    sub_idx = lax.broadcasted_iota(jnp.int32, (rows // sub, sub, block), 1)
    for n in range(width // block):
        cs = slice(n * block, (n + 1) * block)
        xb = xr_sc[:, cs]
        g = jnp.dot(xb.astype(BF16), wg_ref[n], preferred_element_type=F32)
        r = _sigmoid(g[:, :block] + brg_ref[:, cs])
        i = _sigmoid(g[:, block:] + big_ref[:, cs])
        a = jnp.exp2(r * log2_a_max[:, cs])
        b = jnp.sqrt(1.0 - a * a) * (i * xb)
        a3 = a.reshape(rows // sub, sub, block)
        b3 = b.reshape(rows // sub, sub, block)
        d = 1
        while d < sub:
            ok = sub_idx >= d
            b3 = jnp.where(ok, a3 * pltpu.roll(b3, d, axis=1) + b3, b3)
            a3 = jnp.where(ok, a3 * pltpu.roll(a3, d, axis=1), a3)
            d *= 2
        a_sc[:, cs] = a3.reshape(rows, block)
        b_sc[:, cs] = b3.reshape(rows, block)

    def group(gi, h_prev):
        r0 = pl.multiple_of(gi * sub, sub)
        h8 = a_sc[pl.ds(r0, sub), :] * h_prev + b_sc[pl.ds(r0, sub), :]
        b_sc[pl.ds(r0, sub), :] = h8
        return jnp.broadcast_to(h8[sub - 1:sub, :], h8.shape)

    hlast_sc[...] = lax.fori_loop(0, rows // sub, group, hlast_sc[...])

    uy = uy_ref[...].astype(F32)
    k0 = math.sqrt(2.0 / math.pi)
    half_uy = 0.5 * uy
    gelu = half_uy + half_uy * jnp.tanh(uy * (k0 + (k0 * 0.044715) * (uy * uy)))
    y = gelu * b_sc[...]
    o_ref[...] = (y * _rms_scale(y) * gout_ref[...]).astype(BF16)


def _rglru(proj, conv_w, conv_b, w_gates, b_rg, b_ig, lam, g_out, weights_f32, *, batch, seq, width, ux_col):
    nb, block, _ = w_gates.shape
    rows = _tile(seq, 256)
    nt = seq // rows
    ux_blk = ux_col // width
    vec = lambda: pl.BlockSpec((1, width), lambda b, t: (0, 0))
    cast_specs = [_cast_block_spec(*w.shape, batch * nt, lambda b, t: b * nt + t) for w in weights_f32]
    kern = functools.partial(_lru_kernel, block=block, n_cast=len(weights_f32))
    return pl.pallas_call(
        kern,
        grid=(batch, nt),
        in_specs=[
            pl.BlockSpec((rows, width), lambda b, t: (b * nt + t, ux_blk)),
            pl.BlockSpec((rows, width), lambda b, t: (b * nt + t, ux_blk + 1)),
            pl.BlockSpec((CONV_WIDTH, width), lambda b, t: (0, 0)),
            vec(),
            pl.BlockSpec((nb, block, 2 * block), lambda b, t: (0, 0, 0)),
            vec(), vec(), vec(), vec(),
        ] + cast_specs,
        out_specs=[pl.BlockSpec((rows, width), lambda b, t: (b * nt + t, 0))] + cast_specs,
        out_shape=[jax.ShapeDtypeStruct((batch * seq, width), BF16)]
        + [jax.ShapeDtypeStruct(w.shape, BF16) for w in weights_f32],
        scratch_shapes=[
            pltpu.VMEM((V7X_SUBLANES, width), F32),
            pltpu.VMEM((V7X_SUBLANES, width), F32),
            pltpu.VMEM((rows, width), F32),
            pltpu.VMEM((rows, width), F32),
            pltpu.VMEM((rows, width), F32),
        ],
        compiler_params=_params("parallel", "arbitrary"),
        name="rglru",
    )(proj, proj, conv_w, conv_b, w_gates, b_rg, b_ig, lam, g_out, *weights_f32)


def _outproj_kernel(attn_ref, lru_ref, gattn_ref, w_ref, x_ref, h_ref, mixed_sc):
    @pl.when(pl.program_id(1) == 0)
    def _():
        wa = attn_ref.shape[1]
        rows = _tile(attn_ref.shape[0], 256)
        for r in range(0, attn_ref.shape[0], rows):
            a = attn_ref[r:r + rows, :]
            mixed_sc[r:r + rows, :wa] = (a * _rms_scale(a) * gattn_ref[...]).astype(BF16)
        mixed_sc[:, wa:] = lru_ref[...]

    chunk = _tile(h_ref.shape[1], 256)
    for c in range(0, h_ref.shape[1], chunk):
        h_ref[:, c:c + chunk] = x_ref[:, c:c + chunk] + jnp.dot(mixed_sc[...], w_ref[:, c:c + chunk],
                                                                 preferred_element_type=F32)


def _out_projection(attn, lru, g_attn, w_out, x2):
    m, wa = attn.shape
    wl = lru.shape[1]
    d = w_out.shape[1]
    tm = _tile(m, 1024)
    tn = _tile(d, 512)
    return pl.pallas_call(
        _outproj_kernel,
        grid=(m // tm, d // tn),
        in_specs=[
            pl.BlockSpec((tm, wa), lambda i, j: (i, 0)),
            pl.BlockSpec((tm, wl), lambda i, j: (i, 0)),
            pl.BlockSpec((1, wa), lambda i, j: (0, 0)),
            pl.BlockSpec((wa + wl, tn), lambda i, j: (0, j)),
            pl.BlockSpec((tm, tn), lambda i, j: (i, j)),
        ],
        out_specs=pl.BlockSpec((tm, tn), lambda i, j: (i, j)),
        out_shape=jax.ShapeDtypeStruct((m, d), F32),
        scratch_shapes=[pltpu.VMEM((tm, wa + wl), BF16)],
        compiler_params=_params("parallel", "arbitrary"),
        name="out_projection",
    )(attn, lru, g_attn, w_out, x2)


def _ffn_kernel(h_ref, gffn_ref, wg_ref, wu_ref, wd_ref, o_ref, hn_sc):
    @pl.when(pl.program_id(1) == 0)
    def _():
        rows = _tile(h_ref.shape[0], 128)
        for r in range(0, h_ref.shape[0], rows):
            h = h_ref[r:r + rows, :]
            hn_sc[r:r + rows, :] = (h * _rms_scale(h) * gffn_ref[...]).astype(BF16)
            o_ref[r:r + rows, :] = h

    hn = hn_sc[...]
    g = jnp.dot(hn, wg_ref[...], preferred_element_type=F32)
    u = jnp.dot(hn, wu_ref[...], preferred_element_type=F32)
    act = (g * (1.0 / (1.0 + jnp.exp(-g))) * u).astype(BF16)
    chunk = _tile(o_ref.shape[1], 1024)
    for c in range(0, o_ref.shape[1], chunk):
        o_ref[:, c:c + chunk] += jnp.dot(act, wd_ref[:, c:c + chunk], preferred_element_type=F32)


def _ffn(h, g_ffn, w_gate, w_up, w_down):
    m, d = h.shape
    dff = w_gate.shape[1]
    tm = _tile(m, 512)
    tf = _tile(dff, 256)
    return pl.pallas_call(
        _ffn_kernel,
        grid=(m // tm, dff // tf),
        in_specs=[
            pl.BlockSpec((tm, d), lambda i, f: (i, 0)),
            pl.BlockSpec((1, d), lambda i, f: (0, 0)),
            pl.BlockSpec((d, tf), lambda i, f: (0, f)),
            pl.BlockSpec((d, tf), lambda i, f: (0, f)),
            pl.BlockSpec((tf, d), lambda i, f: (f, 0)),
        ],
        out_specs=pl.BlockSpec((tm, d), lambda i, f: (i, 0)),
        out_shape=jax.ShapeDtypeStruct((m, d), F32),
        scratch_shapes=[pltpu.VMEM((tm, d), BF16)],
        compiler_params=_params("parallel", "arbitrary"),
        name="swiglu_ffn",
    )(h, g_ffn, w_gate, w_up, w_down)


def _layer(x, g_mix, w_in, b_f, g_q, g_k, conv_w, conv_b, w_rg, b_rg, w_ig, b_ig, lam,
           g_attn_out, g_lru_out, w_out, g_ffn, w_gate, w_up, w_down):
    batch, seq, d = x.shape
    n_heads = b_f.shape[0]
    head_dim = g_q.shape[0]
    attn_w = n_heads * head_dim
    lru_w = lam.shape[0]
    row = lambda v: v.reshape(1, -1).astype(F32)

    f_lo, f_hi = 3 * attn_w, 3 * attn_w + n_heads
    w_main, w_f = _split_w_in(jnp.swapaxes(w_in, 0, 1), f_lo=f_lo, n_f=n_heads)
    b_f_pad = jnp.pad(b_f.astype(F32), (0, V7X_LANES - n_heads)).reshape(1, V7X_LANES)
    qk_gain = jnp.concatenate([jnp.tile(g_q.astype(F32) * (head_dim ** -0.5 * LOG2E), n_heads),
                               jnp.tile(g_k.astype(F32), n_heads),
                               jnp.ones((w_main.shape[1] - 2 * attn_w,), F32)]).reshape(1, -1)
    w_gates = jnp.concatenate([w_rg, w_ig], axis=-1).astype(BF16)

    x2 = x.reshape(batch * seq, d)
    proj, f_logit = _in_projection(x2, row(g_mix), w_main, w_f, qk_gain,
                                   n_norm_cols=2 * attn_w, head_dim=head_dim)
    c_col = _forget_cumsum(f_logit.reshape(batch, seq, V7X_LANES), b_f_pad)
    qk_bound = (head_dim * BF16_ROUNDING_SLACK) * jnp.max(jnp.abs(qk_gain[:, :attn_w])) * jnp.max(jnp.abs(g_k.astype(F32)))
    qk_bound = jnp.full((1, V7X_LANES), qk_bound, F32)
    attn, w_out16, w_down16 = _attention(
        proj, c_col, qk_bound, [w_out.astype(F32), w_down.astype(F32)],
        batch=batch, seq=seq, n_heads=n_heads, head_dim=head_dim)
    lru, w_gate16, w_up16 = _rglru(
        proj, conv_w.astype(F32), row(conv_b), w_gates, row(b_rg), row(b_ig), row(lam), row(g_lru_out),
        [w_gate.astype(F32), w_up.astype(F32)], batch=batch, seq=seq, width=lru_w, ux_col=3 * attn_w)
    h = _out_projection(attn, lru, row(g_attn_out), w_out16, x2)
    out = _ffn(h, row(g_ffn), w_gate16, w_up16, w_down16)
    return out.reshape(batch, seq, d)


def kernel(x, g_mix, w_in, b_f, g_q, g_k, conv_w, conv_b, w_rg, b_rg, w_ig, b_ig, lam, g_attn_out, g_lru_out,
           w_out, g_ffn, w_gate, w_up, w_down):
    h = x
    for l in range(g_mix.shape[0]):
        h = _layer(h, g_mix[l], w_in[l], b_f[l], g_q[l], g_k[l], conv_w[l], conv_b[l], w_rg[l], b_rg[l],
                   w_ig[l], b_ig[l], lam[l], g_attn_out[l], g_lru_out[l], w_out[l], g_ffn[l], w_gate[l],
                   w_up[l], w_down[l])
    return h
```

```python
import functools
import math

import jax
import jax.numpy as jnp
from jax import lax
from jax.experimental import pallas as pl
from jax.experimental.pallas import tpu as pltpu

EPS = 1e-6
RG_C = 8.0
LOG2E = math.log2(math.e)
EXP2_ZERO = 160.0
BF16_ROUNDING_SLACK = (1.0 + 2.0 ** -8) ** 2 * 1.001
CONV_WIDTH = 4

V7X_LANES = 128
V7X_SUBLANES = 8
V7X_VMEM_LIMIT_BYTES = 58 * 1024 * 1024

F32 = jnp.float32
BF16 = jnp.bfloat16


def _tile(dim, pref):
    t = min(dim, pref)
    while dim % t:
        t //= 2
    return t


def _params(*sem):
    return pltpu.CompilerParams(dimension_semantics=sem, vmem_limit_bytes=V7X_VMEM_LIMIT_BYTES)


def _rms_scale(x):
    return lax.rsqrt(jnp.mean(x * x, axis=-1, keepdims=True) + EPS)


def _sigmoid(z):
    return 0.5 * jnp.tanh(0.5 * z) + 0.5


def _split_w_in_kernel(wt_ref, ft_ref, main_ref, f_ref, *, n_f):
    main_ref[...] = wt_ref[...].T.astype(BF16)

    @pl.when(pl.program_id(0) == 0)
    def _():
        lane = lax.broadcasted_iota(jnp.int32, f_ref.shape, 1)
        f_ref[...] = jnp.where(lane < n_f, ft_ref[...].T, 0.0).astype(BF16)


def _split_w_in(w_in_t, *, f_lo, n_f):
    n_in, d = w_in_t.shape
    n_main = n_in - n_f
    cols = _tile(math.gcd(f_lo, n_main - f_lo), 256)
    assert cols % V7X_LANES == 0 and f_lo + V7X_LANES <= n_in and n_f % V7X_SUBLANES == 0
    return pl.pallas_call(
        functools.partial(_split_w_in_kernel, n_f=n_f),
        grid=(n_main // cols,),
        in_specs=[
            pl.BlockSpec((pl.Element(cols), pl.Element(d)),
                         lambda j: (pl.multiple_of(jnp.where(j * cols < f_lo, j * cols, j * cols + n_f),
                                                   V7X_SUBLANES), 0)),
            pl.BlockSpec((pl.Element(V7X_LANES), pl.Element(d)), lambda j: (f_lo, 0)),
        ],
        out_specs=[pl.BlockSpec((d, cols), lambda j: (0, j)),
                   pl.BlockSpec((d, V7X_LANES), lambda j: (0, 0))],
        out_shape=[jax.ShapeDtypeStruct((d, n_main), BF16),
                   jax.ShapeDtypeStruct((d, V7X_LANES), BF16)],
        compiler_params=_params("arbitrary"),
        name="split_w_in",
    )(w_in_t, w_in_t)


def _inproj_kernel(x_ref, gmix_ref, w_ref, wf_ref, gain_ref, proj_ref, f_ref, xn_sc, *, n_norm_tiles, head_dim):
    j = pl.program_id(1)

    @pl.when(j == 0)
    def _():
        x = x_ref[...]
        xn = (x * _rms_scale(x) * gmix_ref[...]).astype(BF16)
        xn_sc[...] = xn
        f_ref[...] = jnp.dot(xn, wf_ref[...], preferred_element_type=F32)

    @pl.when(j < n_norm_tiles)
    def _():
        step = 2 * head_dim
        for c0 in range(0, proj_ref.shape[1], step):
            y = jnp.dot(xn_sc[...], w_ref[:, c0:c0 + step], preferred_element_type=F32)
            for c in range(c0, c0 + step, head_dim):
                ys = y[:, c - c0:c - c0 + head_dim]
                proj_ref[:, c:c + head_dim] = (ys * _rms_scale(ys) * gain_ref[:, c:c + head_dim]).astype(BF16)

    @pl.when(j >= n_norm_tiles)
    def _():
        proj_ref[...] = jnp.dot(xn_sc[...], w_ref[...], preferred_element_type=F32).astype(BF16)


def _in_projection(x2, g_mix, w_main, w_f, gain, *, n_norm_cols, head_dim):
    m, d = x2.shape
    n = w_main.shape[1]
    tm = _tile(m, 512)
    tn = _tile(math.gcd(n, n_norm_cols), 1024)
    kern = functools.partial(_inproj_kernel, n_norm_tiles=n_norm_cols // tn, head_dim=head_dim)
    return pl.pallas_call(
        kern,
        grid=(m // tm, n // tn),
        in_specs=[
            pl.BlockSpec((tm, d), lambda i, j: (i, 0)),
            pl.BlockSpec((1, d), lambda i, j: (0, 0)),
            pl.BlockSpec((d, tn), lambda i, j: (0, j)),
            pl.BlockSpec((d, V7X_LANES), lambda i, j: (0, 0)),
            pl.BlockSpec((1, tn), lambda i, j: (0, j)),
        ],
        out_specs=[
            pl.BlockSpec((tm, tn), lambda i, j: (i, j)),
            pl.BlockSpec((tm, V7X_LANES), lambda i, j: (i, 0)),
        ],
        out_shape=[
            jax.ShapeDtypeStruct((m, n), BF16),
            jax.ShapeDtypeStruct((m, V7X_LANES), F32),
        ],
        scratch_shapes=[pltpu.VMEM((tm, d), BF16)],
        compiler_params=_params("parallel", "arbitrary"),
        name="in_projection",
    )(x2, g_mix, w_main, w_f, gain)


def _forget_cumsum_kernel(f_ref, bf_ref, c_ref):
    z = f_ref[0] + bf_ref[...]
    c = jnp.minimum(z, 0.0) - jnp.log1p(jnp.exp(-jnp.abs(z)))
    row = lax.broadcasted_iota(jnp.int32, c.shape, 0)
    d = 1
    while d < c.shape[0]:
        c = c + jnp.where(row >= d, pltpu.roll(c, d, axis=0), 0.0)
        d *= 2
    c_ref[0] = c


def _forget_cumsum(f3, b_f_pad):
    b, s, lanes = f3.shape
    return pl.pallas_call(
        _forget_cumsum_kernel,
        grid=(b,),
        in_specs=[
            pl.BlockSpec((1, s, lanes), lambda i: (i, 0, 0)),
            pl.BlockSpec((1, lanes), lambda i: (0, 0)),
        ],
        out_specs=pl.BlockSpec((1, s, lanes), lambda i: (i, 0, 0)),
        out_shape=jax.ShapeDtypeStruct((b, s, lanes), F32),
        compiler_params=_params("parallel"),
        name="forget_cumsum",
    )(f3, b_f_pad)


def _bias_columns(c, ones_first):
    hi = c.astype(BF16).astype(F32)
    mid = (c - hi).astype(BF16).astype(F32)
    lo = (c - hi - mid).astype(BF16).astype(F32)
    lane = lax.broadcasted_iota(jnp.int32, (c.shape[0], V7X_LANES), 1)
    first_c, first_one = (3, 0) if ones_first else (0, 3)
    cols = jnp.where(lane == first_c, hi, jnp.where(lane == first_c + 1, mid, jnp.where(lane == first_c + 2, lo, 0.0)))
    ones = (lane >= first_one) & (lane < first_one + 3)
    return jnp.where(ones, 1.0, cols).astype(BF16)


def _attn_kernel(*refs, tile, chunk, n_cast):
    q_ref, k_ref, v_ref, c_ref, bound_ref = refs[:5]
    w32_refs = refs[5:5 + n_cast]
    o_ref = refs[5 + n_cast]
    w16_refs = refs[6 + n_cast:6 + 2 * n_cast]
    qa_sc, ka_sc, va_sc, m_sc, acc_sc, s0_sc, s1_sc, s2_sc = refs[6 + 2 * n_cast:]

    h = pl.program_id(1)
    qi = pl.program_id(2)
    seq, hd = k_ref.shape
    tq = 2 * tile
    lanes = m_sc.shape[1]

    def head_c(r0, n):
        cc = c_ref[0, pl.ds(r0, n), :]
        lane = lax.broadcasted_iota(jnp.int32, cc.shape, 1)
        return LOG2E * jnp.sum(jnp.where(lane == h, cc, 0.0), axis=1, keepdims=True)

    @pl.when(qi == 0)
    def _():
        def fill(ci, carry):
            r0 = pl.multiple_of(ci * chunk, chunk)
            ka_sc[pl.ds(r0, chunk), :hd] = k_ref[pl.ds(r0, chunk), :]
            ka_sc[pl.ds(r0, chunk), hd:] = _bias_columns(-head_c(r0, chunk), ones_first=True)
            va_sc[pl.ds(r0, chunk), :hd] = v_ref[pl.ds(r0, chunk), :]
            va_sc[pl.ds(r0, chunk), hd:] = jnp.ones((chunk, va_sc.shape[1] - hd), BF16)
            return carry
        lax.fori_loop(0, seq // chunk, fill, 0)

    qa_sc[:, :hd] = q_ref[...]
    qa_sc[:, hd:] = _bias_columns(head_c(pl.multiple_of(qi * tq, tq), tq), ones_first=False)
    m_sc[...] = jnp.full(m_sc.shape, -jnp.inf, F32)
    acc_sc[...] = jnp.zeros(acc_sc.shape, F32)

    def logits(sub, kb):
        k0 = pl.multiple_of(kb * tile, tile)
        return lax.dot_general(qa_sc[sub * tile:(sub + 1) * tile, :], ka_sc[pl.ds(k0, tile), :],
                               (((1,), (1,)), ((), ())), preferred_element_type=F32)

    def update(sub, *blocks):
        rows = slice(sub * tile, (sub + 1) * tile)
        m_prev = m_sc[rows, :]
        m_new = m_prev
        logit_blocks = []
        for s, kb, masked in blocks:
            if masked:
                r = lax.broadcasted_iota(jnp.int32, s.shape, 0)
                c = lax.broadcasted_iota(jnp.int32, s.shape, 1)
                s = jnp.where(c <= r, s, -jnp.inf)
            logit_blocks.append(s)
            m_new = jnp.maximum(m_new, jnp.max(s, axis=1, keepdims=True))
        m_wide = jnp.tile(m_new, (1, tile // lanes))
        pv = None
        for s, (_, kb, _) in zip(logit_blocks, blocks):
            k0 = pl.multiple_of(kb * tile, tile)
            term = jnp.dot(jnp.exp2(s - m_wide).astype(BF16), va_sc[pl.ds(k0, tile), :],
                           preferred_element_type=F32)
            pv = term if pv is None else pv + term
        alpha = jnp.exp2(m_prev - m_new)
        acc_sc[rows, :] = jnp.tile(alpha, (1, acc_sc.shape[1] // lanes)) * acc_sc[rows, :] + pv
        m_sc[rows, :] = m_new

    kd = 2 * qi
    n_blocks = seq // tile
    c_end = c_ref[0, pl.ds(tile - 1, n_blocks, stride=tile), :]
    lane = lax.broadcasted_iota(jnp.int32, c_end.shape, 1)
    c_end = LOG2E * jnp.sum(jnp.where(lane == h, c_end, 0.0), axis=1, keepdims=True)
    c_start = head_c(pl.multiple_of(qi * tq, tq), V7X_SUBLANES)[0:1, :]
    kb_idx = lax.broadcasted_iota(jnp.int32, c_end.shape, 0)
    dead = (2.0 * bound_ref[0:1, 0:1] + (c_start - c_end) <= -EXP2_ZERO) & (kb_idx < kd)
    kb0 = 2 * (jnp.sum(dead.astype(jnp.int32)) // 2)
    pairs = qi - kb0 // 2

    for sub in range(2):
        s0_sc[sub] = logits(sub, kb0)
    s2_sc[...] = logits(1, kd + 1)

    def block_pair(kb):
        for sub in range(2):
            s1_sc[sub] = logits(sub, kb + 1)
        for sub in range(2):
            update(sub, (s0_sc[sub], kb, False))
        for sub in range(2):
            s0_sc[sub] = logits(sub, kb + 2)
        for sub in range(2):
            update(sub, (s1_sc[sub], kb + 1, False))

    def two_pairs(it, carry):
        block_pair(kb0 + 4 * it)
        block_pair(kb0 + 4 * it + 2)
        return carry

    def one_pair(it, carry):
        block_pair(kb0 + 4 * (pairs // 2))
        return carry

    lax.fori_loop(0, pairs // 2, two_pairs, 0)
    lax.fori_loop(0, pairs % 2, one_pair, 0)
    update(0, (s0_sc[0], kd, True))
    update(1, (s0_sc[1], kd, False), (s2_sc[...], kd + 1, True))
    o_ref[...] = acc_sc[:, :hd] / acc_sc[:, hd:2 * hd]
    for w32_ref, w16_ref in zip(w32_refs, w16_refs):
        w16_ref[...] = w32_ref[...].astype(BF16)


def _cast_block_spec(rows, cols, n_steps, step_of):
    bf16_rows = 2 * V7X_SUBLANES
    br = next(r for r in range(bf16_rows, rows + 1, bf16_rows) if rows % r == 0 and rows // r <= n_steps)
    last = rows // br - 1
    return pl.BlockSpec((br, cols), lambda *idx: (jnp.minimum(step_of(*idx), last), 0))


def _attention(proj, c_col, qk_bound, weights_f32, *, batch, seq, n_heads, head_dim):
    assert head_dim == V7X_LANES
    tile = _tile(seq, 512)
    tq = 2 * tile
    assert seq % tq == 0
    nq = seq // tq
    attn_w = n_heads * head_dim
    n_steps = batch * n_heads * nq
    step_of = lambda b, h, i: (b * n_heads + h) * nq + i
    cast_specs = [_cast_block_spec(*w.shape, n_steps, step_of) for w in weights_f32]
    kern = functools.partial(_attn_kernel, tile=tile, chunk=tile, n_cast=len(weights_f32))
    return pl.pallas_call(
        kern,
        grid=(batch, n_heads, nq),
        in_specs=[
            pl.BlockSpec((tq, head_dim), lambda b, h, i: (b * nq + i, h)),
            pl.BlockSpec((seq, head_dim), lambda b, h, i: (b, n_heads + h)),
            pl.BlockSpec((seq, head_dim), lambda b, h, i: (b, 2 * n_heads + h)),
            pl.BlockSpec((1, seq, V7X_LANES), lambda b, h, i: (b, 0, 0)),
            pl.BlockSpec((1, V7X_LANES), lambda b, h, i: (0, 0)),
        ] + cast_specs,
        out_specs=[pl.BlockSpec((tq, head_dim), lambda b, h, i: (b * nq + i, h))] + cast_specs,
        out_shape=[jax.ShapeDtypeStruct((batch * seq, attn_w), F32)]
        + [jax.ShapeDtypeStruct(w.shape, BF16) for w in weights_f32],
        scratch_shapes=[
            pltpu.VMEM((tq, 2 * head_dim), BF16),
            pltpu.VMEM((seq, 2 * head_dim), BF16),
            pltpu.VMEM((seq, 2 * head_dim), BF16),
            pltpu.VMEM((tq, V7X_LANES), F32),
            pltpu.VMEM((tq, 2 * head_dim), F32),
            pltpu.VMEM((2, tile, tile), F32),
            pltpu.VMEM((2, tile, tile), F32),
            pltpu.VMEM((tile, tile), F32),
        ],
        compiler_params=_params("parallel", "parallel", "arbitrary"),
        name="fox_attention",
    )(proj, proj, proj, c_col, qk_bound, *weights_f32)


def _lru_kernel(*refs, block, n_cast):
    ux_ref, uy_ref, cw_ref, cb_ref, wg_ref, brg_ref, big_ref, lam_ref, gout_ref = refs[:9]
    w32_refs = refs[9:9 + n_cast]
    o_ref = refs[9 + n_cast]
    w16_refs = refs[10 + n_cast:10 + 2 * n_cast]
    tail_sc, hlast_sc, xr_sc, a_sc, b_sc = refs[10 + 2 * n_cast:]
    for w32_ref, w16_ref in zip(w32_refs, w16_refs):
        w16_ref[...] = w32_ref[...].astype(BF16)

    t = pl.program_id(1)
    rows, width = xr_sc.shape
    sub = V7X_SUBLANES

    @pl.when(t == 0)
    def _():
        tail_sc[...] = jnp.zeros(tail_sc.shape, F32)
        hlast_sc[...] = jnp.zeros(hlast_sc.shape, F32)

    u = ux_ref[...].astype(F32)
    cw = cw_ref[...]
    cb = cb_ref[...]
    xr = cw[CONV_WIDTH - 1:CONV_WIDTH, :] * u + cb
    for d in range(1, CONV_WIDTH):
        xr = xr + cw[CONV_WIDTH - 1 - d:CONV_WIDTH - d, :] * pltpu.roll(u, d, axis=0)
    xr_sc[...] = xr
    u_head = u[0:sub, :]
    tail = tail_sc[...]
    row8 = lax.broadcasted_iota(jnp.int32, (sub, width), 0)
    xr_head = cw[CONV_WIDTH - 1:CONV_WIDTH, :] * u_head + cb
    for d in range(1, CONV_WIDTH):
        shifted = jnp.where(row8 < d, pltpu.roll(tail, d, axis=0), pltpu.roll(u_head, d, axis=0))
        xr_head = xr_head + cw[CONV_WIDTH - 1 - d:CONV_WIDTH - d, :] * shifted
    xr_sc[0:sub, :] = xr_head
    tail_sc[...] = u[rows - sub:rows, :]

    lam = lam_ref[...]
    log2_a_max = (-RG_C * LOG2E) * (jnp.maximum(-lam, 0.0) + jnp.log1p(jnp.exp(-jnp.abs(lam))))

    sub_idx = lax.broadcasted_iota(jnp.int32, (rows // sub, sub, block), 1)
    for n in range(width // block):
        cs = slice(n * block, (n + 1) * block)
        xb = xr_sc[:, cs]
        g = jnp.dot(xb.astype(BF16), wg_ref[n], preferred_element_type=F32)
        r = _sigmoid(g[:, :block] + brg_ref[:, cs])
        i = _sigmoid(g[:, block:] + big_ref[:, cs])
        a = jnp.exp2(r * log2_a_max[:, cs])
        b = jnp.sqrt(1.0 - a * a) * (i * xb)
        a3 = a.reshape(rows // sub, sub, block)
        b3 = b.reshape(rows // sub, sub, block)
        d = 1
        while d < sub:
            ok = sub_idx >= d
            b3 = jnp.where(ok, a3 * pltpu.roll(b3, d, axis=1) + b3, b3)
            a3 = jnp.where(ok, a3 * pltpu.roll(a3, d, axis=1), a3)
            d *= 2
        a_sc[:, cs] = a3.reshape(rows, block)
        b_sc[:, cs] = b3.reshape(rows, block)

    def group(gi, h_prev):
        r0 = pl.multiple_of(gi * sub, sub)
        h8 = a_sc[pl.ds(r0, sub), :] * h_prev + b_sc[pl.ds(r0, sub), :]
        b_sc[pl.ds(r0, sub), :] = h8
        return jnp.broadcast_to(h8[sub - 1:sub, :], h8.shape)

    hlast_sc[...] = lax.fori_loop(0, rows // sub, group, hlast_sc[...])

    uy = uy_ref[...].astype(F32)
    k0 = math.sqrt(2.0 / math.pi)
    half_uy = 0.5 * uy
    gelu = half_uy + half_uy * jnp.tanh(uy * (k0 + (k0 * 0.044715) * (uy * uy)))
    y = gelu * b_sc[...]
    o_ref[...] = (y * _rms_scale(y) * gout_ref[...]).astype(BF16)


def _rglru(proj, conv_w, conv_b, w_gates, b_rg, b_ig, lam, g_out, weights_f32, *, batch, seq, width, ux_col):
    nb, block, _ = w_gates.shape
    rows = _tile(seq, 256)
    nt = seq // rows
    ux_blk = ux_col // width
    vec = lambda: pl.BlockSpec((1, width), lambda b, t: (0, 0))
    cast_specs = [_cast_block_spec(*w.shape, batch * nt, lambda b, t: b * nt + t) for w in weights_f32]
    kern = functools.partial(_lru_kernel, block=block, n_cast=len(weights_f32))
    return pl.pallas_call(
        kern,
        grid=(batch, nt),
        in_specs=[
            pl.BlockSpec((rows, width), lambda b, t: (b * nt + t, ux_blk)),
            pl.BlockSpec((rows, width), lambda b, t: (b * nt + t, ux_blk + 1)),
            pl.BlockSpec((CONV_WIDTH, width), lambda b, t: (0, 0)),
            vec(),
            pl.BlockSpec((nb, block, 2 * block), lambda b, t: (0, 0, 0)),
            vec(), vec(), vec(), vec(),
        ] + cast_specs,
        out_specs=[pl.BlockSpec((rows, width), lambda b, t: (b * nt + t, 0))] + cast_specs,
        out_shape=[jax.ShapeDtypeStruct((batch * seq, width), BF16)]
        + [jax.ShapeDtypeStruct(w.shape, BF16) for w in weights_f32],
        scratch_shapes=[
            pltpu.VMEM((V7X_SUBLANES, width), F32),
            pltpu.VMEM((V7X_SUBLANES, width), F32),
            pltpu.VMEM((rows, width), F32),
            pltpu.VMEM((rows, width), F32),
            pltpu.VMEM((rows, width), F32),
        ],
        compiler_params=_params("parallel", "arbitrary"),
        name="rglru",
    )(proj, proj, conv_w, conv_b, w_gates, b_rg, b_ig, lam, g_out, *weights_f32)


def _outproj_kernel(attn_ref, lru_ref, gattn_ref, w_ref, x_ref, h_ref, mixed_sc):
    @pl.when(pl.program_id(1) == 0)
    def _():
        wa = attn_ref.shape[1]
        rows = _tile(attn_ref.shape[0], 256)
        for r in range(0, attn_ref.shape[0], rows):
            a = attn_ref[r:r + rows, :]
            mixed_sc[r:r + rows, :wa] = (a * _rms_scale(a) * gattn_ref[...]).astype(BF16)
        mixed_sc[:, wa:] = lru_ref[...]

    chunk = _tile(h_ref.shape[1], 256)
    for c in range(0, h_ref.shape[1], chunk):
        h_ref[:, c:c + chunk] = x_ref[:, c:c + chunk] + jnp.dot(mixed_sc[...], w_ref[:, c:c + chunk],
                                                                 preferred_element_type=F32)


def _out_projection(attn, lru, g_attn, w_out, x2):
    m, wa = attn.shape
    wl = lru.shape[1]
    d = w_out.shape[1]
    tm = _tile(m, 1024)
    tn = _tile(d, 512)
    return pl.pallas_call(
        _outproj_kernel,
        grid=(m // tm, d // tn),
        in_specs=[
            pl.BlockSpec((tm, wa), lambda i, j: (i, 0)),
            pl.BlockSpec((tm, wl), lambda i, j: (i, 0)),
            pl.BlockSpec((1, wa), lambda i, j: (0, 0)),
            pl.BlockSpec((wa + wl, tn), lambda i, j: (0, j)),
            pl.BlockSpec((tm, tn), lambda i, j: (i, j)),
        ],
        out_specs=pl.BlockSpec((tm, tn), lambda i, j: (i, j)),
        out_shape=jax.ShapeDtypeStruct((m, d), F32),
        scratch_shapes=[pltpu.VMEM((tm, wa + wl), BF16)],
        compiler_params=_params("parallel", "arbitrary"),
        name="out_projection",
    )(attn, lru, g_attn, w_out, x2)


def _ffn_kernel(h_ref, gffn_ref, wg_ref, wu_ref, wd_ref, o_ref, hn_sc, *, last_cols):
    @pl.when(pl.program_id(1) == 0)
    def _():
        rows = _tile(h_ref.shape[0], 128)
        for r in range(0, h_ref.shape[0], rows):
            h = h_ref[r:r + rows, :]
            hn_sc[r:r + rows, :] = (h * _rms_scale(h) * gffn_ref[...]).astype(BF16)
            o_ref[r:r + rows, :] = h

    def step(cols):
        hn = hn_sc[...]
        g = jnp.dot(hn, wg_ref[:, :cols], preferred_element_type=F32)
        u = jnp.dot(hn, wu_ref[:, :cols], preferred_element_type=F32)
        act = (g * (1.0 / (1.0 + jnp.exp(-g))) * u).astype(BF16)
        chunk = _tile(o_ref.shape[1], 512)
        for c in range(0, o_ref.shape[1], chunk):
            o_ref[:, c:c + chunk] += jnp.dot(act, wd_ref[:cols, c:c + chunk], preferred_element_type=F32)

    tf = wg_ref.shape[1]
    if last_cols == tf:
        step(tf)
    else:
        last = pl.num_programs(1) - 1
        pl.when(pl.program_id(1) < last)(lambda: step(tf))
        pl.when(pl.program_id(1) == last)(lambda: step(last_cols))


def _ffn(h, g_ffn, w_gate, w_up, w_down):
    m, d = h.shape
    dff = w_gate.shape[1]
    tm = _tile(m, 512)
    tf = min(512, dff)
    nf = pl.cdiv(dff, tf)
    return pl.pallas_call(
        functools.partial(_ffn_kernel, last_cols=dff - (nf - 1) * tf),
        grid=(m // tm, nf),
        in_specs=[
            pl.BlockSpec((tm, d), lambda i, f: (i, 0), pipeline_mode=pl.Buffered(1)),
            pl.BlockSpec((1, d), lambda i, f: (0, 0)),
            pl.BlockSpec((d, tf), lambda i, f: (0, f)),
            pl.BlockSpec((d, tf), lambda i, f: (0, f)),
            pl.BlockSpec((tf, d), lambda i, f: (f, 0)),
        ],
        out_specs=pl.BlockSpec((tm, d), lambda i, f: (i, 0)),
        out_shape=jax.ShapeDtypeStruct((m, d), F32),
        scratch_shapes=[pltpu.VMEM((tm, d), BF16)],
        compiler_params=_params("parallel", "arbitrary"),
        name="swiglu_ffn",
    )(h, g_ffn, w_gate, w_up, w_down)


def _layer(x, g_mix, w_in, b_f, g_q, g_k, conv_w, conv_b, w_rg, b_rg, w_ig, b_ig, lam,
           g_attn_out, g_lru_out, w_out, g_ffn, w_gate, w_up, w_down):
    batch, seq, d = x.shape
    n_heads = b_f.shape[0]
    head_dim = g_q.shape[0]
    attn_w = n_heads * head_dim
    lru_w = lam.shape[0]
    row = lambda v: v.reshape(1, -1).astype(F32)

    f_lo, f_hi = 3 * attn_w, 3 * attn_w + n_heads
    w_main, w_f = _split_w_in(jnp.swapaxes(w_in, 0, 1), f_lo=f_lo, n_f=n_heads)
    b_f_pad = jnp.pad(b_f.astype(F32), (0, V7X_LANES - n_heads)).reshape(1, V7X_LANES)
    qk_gain = jnp.concatenate([jnp.tile(g_q.astype(F32) * (head_dim ** -0.5 * LOG2E), n_heads),
                               jnp.tile(g_k.astype(F32), n_heads),
                               jnp.ones((w_main.shape[1] - 2 * attn_w,), F32)]).reshape(1, -1)
    w_gates = jnp.concatenate([w_rg, w_ig], axis=-1).astype(BF16)

    x2 = x.reshape(batch * seq, d)
    proj, f_logit = _in_projection(x2, row(g_mix), w_main, w_f, qk_gain,
                                   n_norm_cols=2 * attn_w, head_dim=head_dim)
    c_col = _forget_cumsum(f_logit.reshape(batch, seq, V7X_LANES), b_f_pad)
    qk_bound = (head_dim * BF16_ROUNDING_SLACK) * jnp.max(jnp.abs(qk_gain[:, :attn_w])) * jnp.max(jnp.abs(g_k.astype(F32)))
    qk_bound = jnp.full((1, V7X_LANES), qk_bound, F32)
    attn, w_out16, w_down16 = _attention(
        proj, c_col, qk_bound, [w_out.astype(F32), w_down.astype(F32)],
        batch=batch, seq=seq, n_heads=n_heads, head_dim=head_dim)
    lru, w_gate16, w_up16 = _rglru(
        proj, conv_w.astype(F32), row(conv_b), w_gates, row(b_rg), row(b_ig), row(lam), row(g_lru_out),
        [w_gate.astype(F32), w_up.astype(F32)], batch=batch, seq=seq, width=lru_w, ux_col=3 * attn_w)
    h = _out_projection(attn, lru, row(g_attn_out), w_out16, x2)
    out = _ffn(h, row(g_ffn), w_gate16, w_up16, w_down16)
    return out.reshape(batch, seq, d)


def kernel(x, g_mix, w_in, b_f, g_q, g_k, conv_w, conv_b, w_rg, b_rg, w_ig, b_ig, lam, g_attn_out, g_lru_out,
           w_out, g_ffn, w_gate, w_up, w_down):
    h = x
    for l in range(g_mix.shape[0]):
        h = _layer(h, g_mix[l], w_in[l], b_f[l], g_q[l], g_k[l], conv_w[l], conv_b[l], w_rg[l], b_rg[l],
                   w_ig[l], b_ig[l], lam[l], g_attn_out[l], g_lru_out[l], w_out[l], g_ffn[l], w_gate[l],
                   w_up[l], w_down[l])
    return h
```

```python
import functools
import math

import jax
import jax.numpy as jnp
from jax import lax
from jax.experimental import pallas as pl
from jax.experimental.pallas import tpu as pltpu

EPS = 1e-6
RG_C = 8.0
LOG2E = math.log2(math.e)
EXP2_ZERO = 160.0
BF16_ROUNDING_SLACK = (1.0 + 2.0 ** -8) ** 2 * 1.001
CONV_WIDTH = 4

V7X_LANES = 128
V7X_SUBLANES = 8
V7X_VMEM_LIMIT_BYTES = 58 * 1024 * 1024

F32 = jnp.float32
BF16 = jnp.bfloat16


def _tile(dim, pref):
    t = min(dim, pref)
    while dim % t:
        t //= 2
    return t


def _params(*sem):
    return pltpu.CompilerParams(dimension_semantics=sem, vmem_limit_bytes=V7X_VMEM_LIMIT_BYTES)


def _rms_scale(x):
    return lax.rsqrt(jnp.mean(x * x, axis=-1, keepdims=True) + EPS)


def _sigmoid(z):
    return 0.5 * jnp.tanh(0.5 * z) + 0.5


def _split_w_in_kernel(wt_ref, ft_ref, main_ref, f_ref, *, n_f):
    main_ref[...] = wt_ref[...].T.astype(BF16)

    @pl.when(pl.program_id(0) == 0)
    def _():
        lane = lax.broadcasted_iota(jnp.int32, f_ref.shape, 1)
        f_ref[...] = jnp.where(lane < n_f, ft_ref[...].T, 0.0).astype(BF16)


def _split_w_in(w_in_t, *, f_lo, n_f):
    n_in, d = w_in_t.shape
    n_main = n_in - n_f
    cols = _tile(math.gcd(f_lo, n_main - f_lo), 256)
    assert cols % V7X_LANES == 0 and f_lo + V7X_LANES <= n_in and n_f % V7X_SUBLANES == 0
    return pl.pallas_call(
        functools.partial(_split_w_in_kernel, n_f=n_f),
        grid=(n_main // cols,),
        in_specs=[
            pl.BlockSpec((pl.Element(cols), pl.Element(d)),
                         lambda j: (pl.multiple_of(jnp.where(j * cols < f_lo, j * cols, j * cols + n_f),
                                                   V7X_SUBLANES), 0)),
            pl.BlockSpec((pl.Element(V7X_LANES), pl.Element(d)), lambda j: (f_lo, 0)),
        ],
        out_specs=[pl.BlockSpec((d, cols), lambda j: (0, j)),
                   pl.BlockSpec((d, V7X_LANES), lambda j: (0, 0))],
        out_shape=[jax.ShapeDtypeStruct((d, n_main), BF16),
                   jax.ShapeDtypeStruct((d, V7X_LANES), BF16)],
        compiler_params=_params("arbitrary"),
        name="split_w_in",
    )(w_in_t, w_in_t)


def _inproj_kernel(x_ref, gmix_ref, w_ref, wf_ref, gain_ref, proj_ref, f_ref, xn_sc, *, n_norm_tiles, head_dim):
    j = pl.program_id(1)

    @pl.when(j == 0)
    def _():
        x = x_ref[...]
        xn = (x * _rms_scale(x) * gmix_ref[...]).astype(BF16)
        xn_sc[...] = xn
        f_ref[...] = jnp.dot(xn, wf_ref[...], preferred_element_type=F32)

    @pl.when(j < n_norm_tiles)
    def _():
        step = 2 * head_dim
        for c0 in range(0, proj_ref.shape[1], step):
            y = jnp.dot(xn_sc[...], w_ref[:, c0:c0 + step], preferred_element_type=F32)
            for c in range(c0, c0 + step, head_dim):
                ys = y[:, c - c0:c - c0 + head_dim]
                proj_ref[:, c:c + head_dim] = (ys * _rms_scale(ys) * gain_ref[:, c:c + head_dim]).astype(BF16)

    @pl.when(j >= n_norm_tiles)
    def _():
        step = 2 * head_dim
        for c0 in range(0, proj_ref.shape[1], step):
            proj_ref[:, c0:c0 + step] = jnp.dot(xn_sc[...], w_ref[:, c0:c0 + step],
                                                preferred_element_type=F32).astype(BF16)


def _in_projection(x2, g_mix, w_main, w_f, gain, *, n_norm_cols, head_dim):
    m, d = x2.shape
    n = w_main.shape[1]
    tm = _tile(m, 512)
    tn = _tile(math.gcd(n, n_norm_cols), 1024)
    kern = functools.partial(_inproj_kernel, n_norm_tiles=n_norm_cols // tn, head_dim=head_dim)
    return pl.pallas_call(
        kern,
        grid=(m // tm, n // tn),
        in_specs=[
            pl.BlockSpec((tm, d), lambda i, j: (i, 0)),
            pl.BlockSpec((1, d), lambda i, j: (0, 0)),
            pl.BlockSpec((d, tn), lambda i, j: (0, j)),
            pl.BlockSpec((d, V7X_LANES), lambda i, j: (0, 0)),
            pl.BlockSpec((1, tn), lambda i, j: (0, j)),
        ],
        out_specs=[
            pl.BlockSpec((tm, tn), lambda i, j: (i, j)),
            pl.BlockSpec((tm, V7X_LANES), lambda i, j: (i, 0)),
        ],
        out_shape=[
            jax.ShapeDtypeStruct((m, n), BF16),
            jax.ShapeDtypeStruct((m, V7X_LANES), F32),
        ],
        scratch_shapes=[pltpu.VMEM((tm, d), BF16)],
        compiler_params=_params("parallel", "arbitrary"),
        name="in_projection",
    )(x2, g_mix, w_main, w_f, gain)


def _forget_cumsum_kernel(f_ref, bf_ref, c_ref):
    z = f_ref[0] + bf_ref[...]
    c = jnp.minimum(z, 0.0) - jnp.log1p(jnp.exp(-jnp.abs(z)))
    row = lax.broadcasted_iota(jnp.int32, c.shape, 0)
    d = 1
    while d < c.shape[0]:
        c = c + jnp.where(row >= d, pltpu.roll(c, d, axis=0), 0.0)
        d *= 2
    c_ref[0] = c


def _forget_cumsum(f3, b_f_pad):
    b, s, lanes = f3.shape
    return pl.pallas_call(
        _forget_cumsum_kernel,
        grid=(b,),
        in_specs=[
            pl.BlockSpec((1, s, lanes), lambda i: (i, 0, 0)),
            pl.BlockSpec((1, lanes), lambda i: (0, 0)),
        ],
        out_specs=pl.BlockSpec((1, s, lanes), lambda i: (i, 0, 0)),
        out_shape=jax.ShapeDtypeStruct((b, s, lanes), F32),
        compiler_params=_params("parallel"),
        name="forget_cumsum",
    )(f3, b_f_pad)


def _bias_columns(c, ones_first):
    hi = c.astype(BF16).astype(F32)
    mid = (c - hi).astype(BF16).astype(F32)
    lo = (c - hi - mid).astype(BF16).astype(F32)
    lane = lax.broadcasted_iota(jnp.int32, (c.shape[0], V7X_LANES), 1)
    first_c, first_one = (3, 0) if ones_first else (0, 3)
    cols = jnp.where(lane == first_c, hi, jnp.where(lane == first_c + 1, mid, jnp.where(lane == first_c + 2, lo, 0.0)))
    ones = (lane >= first_one) & (lane < first_one + 3)
    return jnp.where(ones, 1.0, cols).astype(BF16)


def _attn_kernel(*refs, tile, chunk, n_cast):
    q_ref, k_ref, v_ref, c_ref, bound_ref = refs[:5]
    w32_refs = refs[5:5 + n_cast]
    o_ref = refs[5 + n_cast]
    w16_refs = refs[6 + n_cast:6 + 2 * n_cast]
    qa_sc, ka_sc, va_sc, m_sc, acc_sc, s0_sc, s1_sc, s2_sc = refs[6 + 2 * n_cast:]

    h = pl.program_id(1)
    qi = pl.program_id(2)
    seq, hd = k_ref.shape
    tq = 2 * tile
    lanes = m_sc.shape[1]

    def head_c(r0, n):
        cc = c_ref[0, pl.ds(r0, n), :]
        lane = lax.broadcasted_iota(jnp.int32, cc.shape, 1)
        return LOG2E * jnp.sum(jnp.where(lane == h, cc, 0.0), axis=1, keepdims=True)

    @pl.when(qi == 0)
    def _():
        def fill(ci, carry):
            r0 = pl.multiple_of(ci * chunk, chunk)
            ka_sc[pl.ds(r0, chunk), :hd] = k_ref[pl.ds(r0, chunk), :]
            ka_sc[pl.ds(r0, chunk), hd:] = _bias_columns(-head_c(r0, chunk), ones_first=True)
            va_sc[pl.ds(r0, chunk), :hd] = v_ref[pl.ds(r0, chunk), :]
            va_sc[pl.ds(r0, chunk), hd:] = jnp.ones((chunk, va_sc.shape[1] - hd), BF16)
            return carry
        lax.fori_loop(0, seq // chunk, fill, 0)

    qa_sc[:, :hd] = q_ref[...]
    qa_sc[:, hd:] = _bias_columns(head_c(pl.multiple_of(qi * tq, tq), tq), ones_first=False)
    m_sc[...] = jnp.full(m_sc.shape, -jnp.inf, F32)
    acc_sc[...] = jnp.zeros(acc_sc.shape, F32)

    def logits(sub, kb):
        k0 = pl.multiple_of(kb * tile, tile)
        return lax.dot_general(qa_sc[sub * tile:(sub + 1) * tile, :], ka_sc[pl.ds(k0, tile), :],
                               (((1,), (1,)), ((), ())), preferred_element_type=F32)

    def update(sub, *blocks):
        rows = slice(sub * tile, (sub + 1) * tile)
        m_prev = m_sc[rows, :]
        m_new = m_prev
        logit_blocks = []
        for s, kb, masked in blocks:
            if masked:
                r = lax.broadcasted_iota(jnp.int32, s.shape, 0)
                c = lax.broadcasted_iota(jnp.int32, s.shape, 1)
                s = jnp.where(c <= r, s, -jnp.inf)
            logit_blocks.append(s)
            m_new = jnp.maximum(m_new, jnp.max(s, axis=1, keepdims=True))
        m_wide = jnp.tile(m_new, (1, tile // lanes))
        pv = None
        for s, (_, kb, _) in zip(logit_blocks, blocks):
            k0 = pl.multiple_of(kb * tile, tile)
            term = jnp.dot(jnp.exp2(s - m_wide).astype(BF16), va_sc[pl.ds(k0, tile), :],
                           preferred_element_type=F32)
            pv = term if pv is None else pv + term
        alpha = jnp.exp2(m_prev - m_new)
        acc_sc[rows, :] = jnp.tile(alpha, (1, acc_sc.shape[1] // lanes)) * acc_sc[rows, :] + pv
        m_sc[rows, :] = m_new

    kd = 2 * qi
    n_blocks = seq // tile
    c_end = c_ref[0, pl.ds(tile - 1, n_blocks, stride=tile), :]
    lane = lax.broadcasted_iota(jnp.int32, c_end.shape, 1)
    c_end = LOG2E * jnp.sum(jnp.where(lane == h, c_end, 0.0), axis=1, keepdims=True)
    c_start = head_c(pl.multiple_of(qi * tq, tq), V7X_SUBLANES)[0:1, :]
    kb_idx = lax.broadcasted_iota(jnp.int32, c_end.shape, 0)
    dead = (2.0 * bound_ref[0:1, 0:1] + (c_start - c_end) <= -EXP2_ZERO) & (kb_idx < kd)
    kb0 = 2 * (jnp.sum(dead.astype(jnp.int32)) // 2)
    pairs = qi - kb0 // 2

    for sub in range(2):
        s0_sc[sub] = logits(sub, kb0)
    s2_sc[...] = logits(1, kd + 1)

    def block_pair(kb):
        for sub in range(2):
            s1_sc[sub] = logits(sub, kb + 1)
        for sub in range(2):
            update(sub, (s0_sc[sub], kb, False))
        for sub in range(2):
            s0_sc[sub] = logits(sub, kb + 2)
        for sub in range(2):
            update(sub, (s1_sc[sub], kb + 1, False))

    def two_pairs(it, carry):
        block_pair(kb0 + 4 * it)
        block_pair(kb0 + 4 * it + 2)
        return carry

    def one_pair(it, carry):
        block_pair(kb0 + 4 * (pairs // 2))
        return carry

    lax.fori_loop(0, pairs // 2, two_pairs, 0)
    lax.fori_loop(0, pairs % 2, one_pair, 0)
    update(0, (s0_sc[0], kd, True))
    update(1, (s0_sc[1], kd, False), (s2_sc[...], kd + 1, True))
    o_ref[...] = acc_sc[:, :hd] / acc_sc[:, hd:2 * hd]
    for w32_ref, w16_ref in zip(w32_refs, w16_refs):
        w16_ref[...] = w32_ref[...].astype(BF16)


def _cast_block_spec(rows, cols, n_steps, step_of):
    bf16_rows = 2 * V7X_SUBLANES
    br = next(r for r in range(bf16_rows, rows + 1, bf16_rows) if rows % r == 0 and rows // r <= n_steps)
    last = rows // br - 1
    return pl.BlockSpec((br, cols), lambda *idx: (jnp.minimum(step_of(*idx), last), 0))


def _attention(proj, c_col, qk_bound, weights_f32, *, batch, seq, n_heads, head_dim):
    assert head_dim == V7X_LANES
    tile = _tile(seq, 512)
    tq = 2 * tile
    assert seq % tq == 0
    nq = seq // tq
    attn_w = n_heads * head_dim
    n_steps = batch * n_heads * nq
    step_of = lambda b, h, i: (b * n_heads + h) * nq + i
    cast_specs = [_cast_block_spec(*w.shape, n_steps, step_of) for w in weights_f32]
    kern = functools.partial(_attn_kernel, tile=tile, chunk=tile, n_cast=len(weights_f32))
    return pl.pallas_call(
        kern,
        grid=(batch, n_heads, nq),
        in_specs=[
            pl.BlockSpec((tq, head_dim), lambda b, h, i: (b * nq + i, h)),
            pl.BlockSpec((seq, head_dim), lambda b, h, i: (b, n_heads + h)),
            pl.BlockSpec((seq, head_dim), lambda b, h, i: (b, 2 * n_heads + h)),
            pl.BlockSpec((1, seq, V7X_LANES), lambda b, h, i: (b, 0, 0)),
            pl.BlockSpec((1, V7X_LANES), lambda b, h, i: (0, 0)),
        ] + cast_specs,
        out_specs=[pl.BlockSpec((tq, head_dim), lambda b, h, i: (b * nq + i, h))] + cast_specs,
        out_shape=[jax.ShapeDtypeStruct((batch * seq, attn_w), F32)]
        + [jax.ShapeDtypeStruct(w.shape, BF16) for w in weights_f32],
        scratch_shapes=[
            pltpu.VMEM((tq, 2 * head_dim), BF16),
            pltpu.VMEM((seq, 2 * head_dim), BF16),
            pltpu.VMEM((seq, 2 * head_dim), BF16),
            pltpu.VMEM((tq, V7X_LANES), F32),
            pltpu.VMEM((tq, 2 * head_dim), F32),
            pltpu.VMEM((2, tile, tile), F32),
            pltpu.VMEM((2, tile, tile), F32),
            pltpu.VMEM((tile, tile), F32),
        ],
        compiler_params=_params("parallel", "parallel", "arbitrary"),
        name="fox_attention",
    )(proj, proj, proj, c_col, qk_bound, *weights_f32)


def _lru_kernel(*refs, block, n_cast):
    ux_ref, uy_ref, cw_ref, cb_ref, wg_ref, brg_ref, big_ref, lam_ref, gout_ref = refs[:9]
    w32_refs = refs[9:9 + n_cast]
    o_ref = refs[9 + n_cast]
    w16_refs = refs[10 + n_cast:10 + 2 * n_cast]
    tail_sc, hlast_sc, xr_sc, a_sc, b_sc = refs[10 + 2 * n_cast:]
    for w32_ref, w16_ref in zip(w32_refs, w16_refs):
        w16_ref[...] = w32_ref[...].astype(BF16)

    t = pl.program_id(1)
    rows, width = xr_sc.shape
    sub = V7X_SUBLANES

    @pl.when(t == 0)
    def _():
        tail_sc[...] = jnp.zeros(tail_sc.shape, F32)
        hlast_sc[...] = jnp.zeros(hlast_sc.shape, F32)

    u = ux_ref[...].astype(F32)
    cw = cw_ref[...]
    cb = cb_ref[...]
    xr = cw[CONV_WIDTH - 1:CONV_WIDTH, :] * u + cb
    for d in range(1, CONV_WIDTH):
        xr = xr + cw[CONV_WIDTH - 1 - d:CONV_WIDTH - d, :] * pltpu.roll(u, d, axis=0)
    xr_sc[...] = xr
    u_head = u[0:sub, :]
    tail = tail_sc[...]
    row8 = lax.broadcasted_iota(jnp.int32, (sub, width), 0)
    xr_head = cw[CONV_WIDTH - 1:CONV_WIDTH, :] * u_head + cb
    for d in range(1, CONV_WIDTH):
        shifted = jnp.where(row8 < d, pltpu.roll(tail, d, axis=0), pltpu.roll(u_head, d, axis=0))
        xr_head = xr_head + cw[CONV_WIDTH - 1 - d:CONV_WIDTH - d, :] * shifted
    xr_sc[0:sub, :] = xr_head
    tail_sc[...] = u[rows - sub:rows, :]

    lam = lam_ref[...]
    log2_a_max = (-RG_C * LOG2E) * (jnp.maximum(-lam, 0.0) + jnp.log1p(jnp.exp(-jnp.abs(lam))))

    sub_idx = lax.broadcasted_iota(jnp.int32, (rows // sub, sub, block), 1)
    for n in range(width // block):
        cs = slice(n * block, (n + 1) * block)
        xb = xr_sc[:, cs]
        g = jnp.dot(xb.astype(BF16), wg_ref[n], preferred_element_type=F32)
        r = _sigmoid(g[:, :block] + brg_ref[:, cs])
        i = _sigmoid(g[:, block:] + big_ref[:, cs])
        a = jnp.exp2(r * log2_a_max[:, cs])
        one_minus_a2 = 1.0 - a * a
        root = jnp.where(one_minus_a2 > 0.0, one_minus_a2 * lax.rsqrt(one_minus_a2), 0.0)
        b = root * (i * xb)
        a3 = a.reshape(rows // sub, sub, block)
        b3 = b.reshape(rows // sub, sub, block)
        d = 1
        while d < sub:
            ok = sub_idx >= d
            b3 = jnp.where(ok, a3 * pltpu.roll(b3, d, axis=1) + b3, b3)
            a3 = jnp.where(ok, a3 * pltpu.roll(a3, d, axis=1), a3)
            d *= 2
        a_sc[:, cs] = a3.reshape(rows, block)
        b_sc[:, cs] = b3.reshape(rows, block)

    def group(gi, h_prev):
        r0 = pl.multiple_of(gi * sub, sub)
        h8 = a_sc[pl.ds(r0, sub), :] * h_prev + b_sc[pl.ds(r0, sub), :]
        b_sc[pl.ds(r0, sub), :] = h8
        return jnp.broadcast_to(h8[sub - 1:sub, :], h8.shape)

    hlast_sc[...] = lax.fori_loop(0, rows // sub, group, hlast_sc[...])

    uy = uy_ref[...].astype(F32)
    k0 = math.sqrt(2.0 / math.pi)
    half_uy = 0.5 * uy
    gelu = half_uy + half_uy * jnp.tanh(uy * (k0 + (k0 * 0.044715) * (uy * uy)))
    y = gelu * b_sc[...]
    o_ref[...] = (y * _rms_scale(y) * gout_ref[...]).astype(BF16)


def _rglru(proj, conv_w, conv_b, w_gates, b_rg, b_ig, lam, g_out, weights_f32, *, batch, seq, width, ux_col):
    nb, block, _ = w_gates.shape
    rows = _tile(seq, 256)
    nt = seq // rows
    ux_blk = ux_col // width
    vec = lambda: pl.BlockSpec((1, width), lambda b, t: (0, 0))
    cast_specs = [_cast_block_spec(*w.shape, batch * nt, lambda b, t: b * nt + t) for w in weights_f32]
    kern = functools.partial(_lru_kernel, block=block, n_cast=len(weights_f32))
    return pl.pallas_call(
        kern,
        grid=(batch, nt),
        in_specs=[
            pl.BlockSpec((rows, width), lambda b, t: (b * nt + t, ux_blk)),
            pl.BlockSpec((rows, width), lambda b, t: (b * nt + t, ux_blk + 1)),
            pl.BlockSpec((CONV_WIDTH, width), lambda b, t: (0, 0)),
            vec(),
            pl.BlockSpec((nb, block, 2 * block), lambda b, t: (0, 0, 0)),
            vec(), vec(), vec(), vec(),
        ] + cast_specs,
        out_specs=[pl.BlockSpec((rows, width), lambda b, t: (b * nt + t, 0))] + cast_specs,
        out_shape=[jax.ShapeDtypeStruct((batch * seq, width), BF16)]
        + [jax.ShapeDtypeStruct(w.shape, BF16) for w in weights_f32],
        scratch_shapes=[
            pltpu.VMEM((V7X_SUBLANES, width), F32),
            pltpu.VMEM((V7X_SUBLANES, width), F32),
            pltpu.VMEM((rows, width), F32),
            pltpu.VMEM((rows, width), F32),
            pltpu.VMEM((rows, width), F32),
        ],
        compiler_params=_params("parallel", "arbitrary"),
        name="rglru",
    )(proj, proj, conv_w, conv_b, w_gates, b_rg, b_ig, lam, g_out, *weights_f32)


def _outproj_kernel(attn_ref, lru_ref, gattn_ref, w_ref, x_ref, h_ref, mixed_sc):
    @pl.when(pl.program_id(1) == 0)
    def _():
        wa = attn_ref.shape[1]
        rows = _tile(attn_ref.shape[0], 256)
        for r in range(0, attn_ref.shape[0], rows):
            a = attn_ref[r:r + rows, :]
            mixed_sc[r:r + rows, :wa] = (a * _rms_scale(a) * gattn_ref[...]).astype(BF16)
        mixed_sc[:, wa:] = lru_ref[...]

    chunk = _tile(h_ref.shape[1], 256)
    for c in range(0, h_ref.shape[1], chunk):
        h_ref[:, c:c + chunk] = x_ref[:, c:c + chunk] + jnp.dot(mixed_sc[...], w_ref[:, c:c + chunk],
                                                                 preferred_element_type=F32)


def _out_projection(attn, lru, g_attn, w_out, x2):
    m, wa = attn.shape
    wl = lru.shape[1]
    d = w_out.shape[1]
    tm = _tile(m, 1024)
    tn = _tile(d, 512)
    return pl.pallas_call(
        _outproj_kernel,
        grid=(m // tm, d // tn),
        in_specs=[
            pl.BlockSpec((tm, wa), lambda i, j: (i, 0)),
            pl.BlockSpec((tm, wl), lambda i, j: (i, 0)),
            pl.BlockSpec((1, wa), lambda i, j: (0, 0)),
            pl.BlockSpec((wa + wl, tn), lambda i, j: (0, j)),
            pl.BlockSpec((tm, tn), lambda i, j: (i, j)),
        ],
        out_specs=pl.BlockSpec((tm, tn), lambda i, j: (i, j)),
        out_shape=jax.ShapeDtypeStruct((m, d), F32),
        scratch_shapes=[pltpu.VMEM((tm, wa + wl), BF16)],
        compiler_params=_params("parallel", "arbitrary"),
        name="out_projection",
    )(attn, lru, g_attn, w_out, x2)


def _ffn_kernel(h_ref, gffn_ref, wg_ref, wu_ref, wd_ref, o_ref, hn_sc, *, last_cols):
    @pl.when(pl.program_id(1) == 0)
    def _():
        rows = _tile(h_ref.shape[0], 128)
        for r in range(0, h_ref.shape[0], rows):
            h = h_ref[r:r + rows, :]
            hn_sc[r:r + rows, :] = (h * _rms_scale(h) * gffn_ref[...]).astype(BF16)
            o_ref[r:r + rows, :] = h

    def step(cols):
        hn = hn_sc[...]
        g = jnp.dot(hn, wg_ref[:, :cols], preferred_element_type=F32)
        u = jnp.dot(hn, wu_ref[:, :cols], preferred_element_type=F32)
        act = (g * (1.0 / (1.0 + jnp.exp(-g))) * u).astype(BF16)
        chunk = _tile(o_ref.shape[1], 512)
        for c in range(0, o_ref.shape[1], chunk):
            o_ref[:, c:c + chunk] += jnp.dot(act, wd_ref[:cols, c:c + chunk], preferred_element_type=F32)

    tf = wg_ref.shape[1]
    if last_cols == tf:
        step(tf)
    else:
        last = pl.num_programs(1) - 1
        pl.when(pl.program_id(1) < last)(lambda: step(tf))
        pl.when(pl.program_id(1) == last)(lambda: step(last_cols))


def _ffn(h, g_ffn, w_gate, w_up, w_down):
    m, d = h.shape
    dff = w_gate.shape[1]
    tm = _tile(m, 512)
    tf = min(512, dff)
    nf = pl.cdiv(dff, tf)
    return pl.pallas_call(
        functools.partial(_ffn_kernel, last_cols=dff - (nf - 1) * tf),
        grid=(m // tm, nf),
        in_specs=[
            pl.BlockSpec((tm, d), lambda i, f: (i, 0), pipeline_mode=pl.Buffered(1)),
            pl.BlockSpec((1, d), lambda i, f: (0, 0)),
            pl.BlockSpec((d, tf), lambda i, f: (0, f)),
            pl.BlockSpec((d, tf), lambda i, f: (0, f)),
            pl.BlockSpec((tf, d), lambda i, f: (f, 0)),
        ],
        out_specs=pl.BlockSpec((tm, d), lambda i, f: (i, 0)),
        out_shape=jax.ShapeDtypeStruct((m, d), F32),
        scratch_shapes=[pltpu.VMEM((tm, d), BF16)],
        compiler_params=_params("parallel", "arbitrary"),
        name="swiglu_ffn",
    )(h, g_ffn, w_gate, w_up, w_down)


def _layer(x, g_mix, w_in, b_f, g_q, g_k, conv_w, conv_b, w_rg, b_rg, w_ig, b_ig, lam,
           g_attn_out, g_lru_out, w_out, g_ffn, w_gate, w_up, w_down):
    batch, seq, d = x.shape
    n_heads = b_f.shape[0]
    head_dim = g_q.shape[0]
    attn_w = n_heads * head_dim
    lru_w = lam.shape[0]
    row = lambda v: v.reshape(1, -1).astype(F32)

    f_lo, f_hi = 3 * attn_w, 3 * attn_w + n_heads
    w_main, w_f = _split_w_in(jnp.swapaxes(w_in, 0, 1), f_lo=f_lo, n_f=n_heads)
    b_f_pad = jnp.pad(b_f.astype(F32), (0, V7X_LANES - n_heads)).reshape(1, V7X_LANES)
    qk_gain = jnp.concatenate([jnp.tile(g_q.astype(F32) * (head_dim ** -0.5 * LOG2E), n_heads),
                               jnp.tile(g_k.astype(F32), n_heads),
                               jnp.ones((w_main.shape[1] - 2 * attn_w,), F32)]).reshape(1, -1)
    w_gates = jnp.concatenate([w_rg, w_ig], axis=-1).astype(BF16)

    x2 = x.reshape(batch * seq, d)
    proj, f_logit = _in_projection(x2, row(g_mix), w_main, w_f, qk_gain,
                                   n_norm_cols=2 * attn_w, head_dim=head_dim)
    c_col = _forget_cumsum(f_logit.reshape(batch, seq, V7X_LANES), b_f_pad)
    qk_bound = (head_dim * BF16_ROUNDING_SLACK) * jnp.max(jnp.abs(qk_gain[:, :attn_w])) * jnp.max(jnp.abs(g_k.astype(F32)))
    qk_bound = jnp.full((1, V7X_LANES), qk_bound, F32)
    attn, w_out16, w_down16 = _attention(
        proj, c_col, qk_bound, [w_out.astype(F32), w_down.astype(F32)],
        batch=batch, seq=seq, n_heads=n_heads, head_dim=head_dim)
    lru, w_gate16, w_up16 = _rglru(
        proj, conv_w.astype(F32), row(conv_b), w_gates, row(b_rg), row(b_ig), row(lam), row(g_lru_out),
        [w_gate.astype(F32), w_up.astype(F32)], batch=batch, seq=seq, width=lru_w, ux_col=3 * attn_w)
    h = _out_projection(attn, lru, row(g_attn_out), w_out16, x2)
    out = _ffn(h, row(g_ffn), w_gate16, w_up16, w_down16)
    return out.reshape(batch, seq, d)


def kernel(x, g_mix, w_in, b_f, g_q, g_k, conv_w, conv_b, w_rg, b_rg, w_ig, b_ig, lam, g_attn_out, g_lru_out,
           w_out, g_ffn, w_gate, w_up, w_down):
    h = x
    for l in range(g_mix.shape[0]):
        h = _layer(h, g_mix[l], w_in[l], b_f[l], g_q[l], g_k[l], conv_w[l], conv_b[l], w_rg[l], b_rg[l],
                   w_ig[l], b_ig[l], lam[l], g_attn_out[l], g_lru_out[l], w_out[l], g_ffn[l], w_gate[l],
                   w_up[l], w_down[l])
    return h
```
